```python
import jax, jax.numpy as jnp
from jax import lax
import numpy as np

D_MODEL = 1024
BATCH = 8
SEQ = 2048
DEPTH = 4
DEC_BATCH = 128
DEC_SEQ = 1
PAST_LEN = 8192
PAGE_SIZE = 128

N_A_LAYERS = DEPTH // 2
N_B_LAYERS = DEPTH - N_A_LAYERS
D_PLE = 256
D_FF = 2816
EPS = 1e-6
SSM_EXPAND = 2
D_INNER = SSM_EXPAND * D_MODEL
SSM_HEAD_DIM = 64
SSM_HEADS = D_INNER // SSM_HEAD_DIM
SSM_GROUPS = 8
HEADS_PER_GROUP = SSM_HEADS // SSM_GROUPS
D_STATE = 128
CONV_W = 4
CONV_DIM = D_INNER + 2 * SSM_GROUPS * D_STATE
SSD_CHUNK = 128
ATT_HEAD_DIM = 64
ATT_HEADS = D_MODEL // ATT_HEAD_DIM
ATT_KV_HEADS = 4
ATT_GROUP = ATT_HEADS // ATT_KV_HEADS
WINDOW = 128
Q_BLOCK = 128
ROPE_THETA = 10000.0

kernel_name = "yoco_mamba2_swa_sink_macaron_decoder_step"

F32 = jnp.float32


def rmsnorm(x, g):
    xf = x.astype(F32)
    y = xf * lax.rsqrt(jnp.mean(xf * xf, axis=-1, keepdims=True) + EPS)
    return (y * g.astype(F32)).astype(x.dtype)


def swiglu(x, wi, wo):
    gt, up = jnp.split(x @ wi, 2, axis=-1)
    return (jax.nn.silu(gt) * up) @ wo


def rope(x, pos):
    half = x.shape[-1] // 2
    inv = ROPE_THETA ** (-jnp.arange(half, dtype=F32) / half)
    ang = pos.astype(F32)[:, None] * inv[None, :]
    cos = jnp.cos(ang)[None, :, None, :]
    sin = jnp.sin(ang)[None, :, None, :]
    xf = x.astype(F32)
    x1, x2 = xf[..., :half], xf[..., half:]
    return jnp.concatenate([x1 * cos - x2 * sin, x2 * cos + x1 * sin], axis=-1).astype(x.dtype)


def ssd_scan(x, dt, a, bm, cm, h0):
    bsz, t_len = x.shape[:2]
    L = min(SSD_CHUNK, t_len)
    t_pad = -(-t_len // L) * L
    pad = t_pad - t_len
    if pad:
        def pw(arr):
            return jnp.pad(arr, [(0, 0), (0, pad)] + [(0, 0)] * (arr.ndim - 2))
        x, dt, bm, cm = pw(x), pw(dt), pw(bm), pw(cm)
    nc = t_pad // L
    G, E, P, N = SSM_GROUPS, HEADS_PER_GROUP, SSM_HEAD_DIM, D_STATE
    xf = x.astype(F32).reshape(bsz, nc, L, G, E, P)
    dtc = dt.astype(F32).reshape(bsz, nc, L, G, E)
    bc = bm.astype(F32).reshape(bsz, nc, L, G, N)
    cc = cm.astype(F32).reshape(bsz, nc, L, G, N)
    xdt = xf * dtc[..., None]
    cum = jnp.cumsum(dtc * a.reshape(G, E), axis=2)
    cum_t = jnp.moveaxis(cum, 2, -1)
    causal = jnp.tril(jnp.ones((L, L), dtype=bool))
    seg = cum_t[..., :, None] - cum_t[..., None, :]
    decay = jnp.exp(jnp.where(causal, seg, -jnp.inf))
    cb = jnp.einsum('bclgn,bcsgn->bcgls', cc, bc)
    y_intra = jnp.einsum('bcgels,bcsgep->bclgep', cb[:, :, :, None] * decay, xdt)
    cum_last = cum[:, :, -1]
    to_end = jnp.exp(cum_last[:, :, None] - cum)
    s_chunk = jnp.einsum('bclgn,bclgep->bcgepn', bc, xdt * to_end[..., None])

    def step(h, inp):
        s_c, dec_c = inp
        return h * dec_c[..., None, None] + s_c, h

    h_init = h0.astype(F32).reshape(bsz, G, E, P, N)
    h_last, h_in = lax.scan(step, h_init, (jnp.moveaxis(s_chunk, 1, 0), jnp.moveaxis(jnp.exp(cum_last), 1, 0)))
    h_in = jnp.moveaxis(h_in, 0, 1)
    y_inter = jnp.einsum('bclgn,bcgepn->bclgep', cc, h_in) * jnp.exp(cum)[..., None]
    y = (y_intra + y_inter).reshape(bsz, t_pad, SSM_HEADS, P)[:, :t_len]
    return y, h_last.reshape(bsz, SSM_HEADS, P, N)


def mamba2_mixer(u, w_in, conv_w, conv_b, dt_bias, a_log, d_skip, norm_g, w_out, ssm0, conv0):
    bsz, t_len, _ = u.shape
    z, xbc, dt_raw = jnp.split(u @ w_in, [D_INNER, D_INNER + CONV_DIM], axis=-1)
    xbc_full = jnp.concatenate([conv0.astype(xbc.dtype), xbc], axis=1)
    conv = conv_b + sum(xbc_full[:, k:k + t_len] * conv_w[k] for k in range(CONV_W))
    xbc_act = jax.nn.silu(conv)
    xs, bm, cm = jnp.split(xbc_act, [D_INNER, D_INNER + SSM_GROUPS * D_STATE], axis=-1)
    xs = xs.reshape(bsz, t_len, SSM_HEADS, SSM_HEAD_DIM)
    bm = bm.reshape(bsz, t_len, SSM_GROUPS, D_STATE)
    cm = cm.reshape(bsz, t_len, SSM_GROUPS, D_STATE)
    dt = jax.nn.softplus(dt_raw.astype(F32) + dt_bias.astype(F32))
    a = -jnp.exp(a_log.astype(F32))
    y, h_last = ssd_scan(xs, dt, a, bm, cm, ssm0)
    y = y + d_skip.astype(F32)[:, None] * xs.astype(F32)
    gated = y.reshape(bsz, t_len, D_INNER) * jax.nn.silu(z.astype(F32))
    gg = gated.reshape(bsz, t_len, SSM_GROUPS, D_INNER // SSM_GROUPS)
    gg = gg * lax.rsqrt(jnp.mean(gg * gg, axis=-1, keepdims=True) + EPS)
    out = (gg.reshape(bsz, t_len, D_INNER) * norm_g.astype(F32)).astype(u.dtype) @ w_out
    return out, h_last.astype(ssm0.dtype), xbc_full[:, -(CONV_W - 1):].astype(conv0.dtype)


def shared_kv(x, kv_norm, w_kv, k_norm, pos):
    bsz, t_len, _ = x.shape
    k, v = jnp.split(rmsnorm(x, kv_norm) @ w_kv, 2, axis=-1)
    k = k.reshape(bsz, t_len, ATT_KV_HEADS, ATT_HEAD_DIM)
    v = v.reshape(bsz, t_len, ATT_KV_HEADS, ATT_HEAD_DIM)
    return rope(rmsnorm(k, k_norm), pos), v


def window_attention(q, k_all, v_all, hist_valid, sinks):
    bsz, t_len = q.shape[:2]
    qb = Q_BLOCK if t_len % Q_BLOCK == 0 else t_len
    nb = t_len // qb
    kspan = qb + WINDOW
    starts = jnp.arange(nb) * qb
    kidx = starts[:, None] + jnp.arange(kspan)[None, :]
    tq = starts[:, None] + jnp.arange(qb)[None, :]
    kb = jnp.take(k_all, kidx, axis=1)
    vb = jnp.take(v_all, kidx, axis=1)
    qg = q.reshape(bsz, nb, qb, ATT_KV_HEADS, ATT_GROUP, ATT_HEAD_DIM)
    s = jnp.einsum('bnqkgd,bnskd->bnkgqs', qg, kb).astype(F32) * (ATT_HEAD_DIM ** -0.5)
    rel = (WINDOW + tq)[:, :, None] - kidx[:, None, :]
    mask = (rel >= 0) & (rel <= WINDOW)
    if not hist_valid:
        mask = mask & (kidx[:, None, :] >= WINDOW)
    s = jnp.where(mask[None, :, None, None], s, -jnp.inf)
    sink = jnp.broadcast_to(sinks.astype(F32).reshape(ATT_KV_HEADS, ATT_GROUP)[None, None, :, :, None, None], s.shape[:-1] + (1,))
    prob = jax.nn.softmax(jnp.concatenate([s, sink], axis=-1), axis=-1)[..., :-1]
    o = jnp.einsum('bnkgqs,bnskd->bnqkgd', prob.astype(vb.dtype), vb)
    return o.reshape(bsz, t_len, ATT_HEADS * ATT_HEAD_DIM)


def trunk(x, p, ssm0, conv0, k_hist, v_hist, pos0, hist_valid, w):
    bsz, t_len, _ = x.shape
    pos = pos0 + jnp.arange(t_len)
    new_ssm, new_conv = [], []
    k_all = v_all = None
    for i in range(DEPTH):
        x = x + 0.5 * swiglu(rmsnorm(x, w['ffn1_norm'][i]), w['ffn1_wi'][i], w['ffn1_wo'][i])
        h = rmsnorm(x, w['mix_norm'][i])
        if i < N_A_LAYERS:
            out, h_last, c_last = mamba2_mixer(h, w['ssm_in'][i], w['ssm_conv_w'][i], w['ssm_conv_b'][i], w['ssm_dt_bias'][i], w['ssm_a_log'][i], w['ssm_d'][i], w['ssm_norm'][i], w['ssm_out'][i], ssm0[i], conv0[i])
            new_ssm.append(h_last)
            new_conv.append(c_last)
        else:
            j = i - N_A_LAYERS
            q = (h @ w['w_q'][j]).reshape(bsz, t_len, ATT_HEADS, ATT_HEAD_DIM)
            q = rope(rmsnorm(q, w['q_norm'][j]), pos)
            out = window_attention(q, k_all, v_all, hist_valid, w['attn_sinks'][j]) @ w['w_o'][j]
        x = x + out
        x = x + 0.5 * swiglu(rmsnorm(x, w['ffn2_norm'][i]), w['ffn2_wi'][i], w['ffn2_wo'][i])
        gate = jax.nn.sigmoid(rmsnorm(x, w['ple_norm'][i]) @ w['ple_gate'][i])
        x = x + gate * (p[i] @ w['ple_proj'][i])
        if i == N_A_LAYERS - 1:
            k_new, v_new = shared_kv(x, w['kv_norm'], w['w_kv'], w['k_norm'], pos)
            k_all = jnp.concatenate([k_hist.astype(k_new.dtype), k_new], axis=1)
            v_all = jnp.concatenate([v_hist.astype(v_new.dtype), v_new], axis=1)
    return x, jnp.stack(new_ssm), jnp.stack(new_conv), k_all[:, -WINDOW:], v_all[:, -WINDOW:]


def setup_inputs(seed: int = 0) -> dict:
    key = jax.random.key(seed)
    ks = list(jax.random.split(key, 48))
    cnt = [0]

    def nk():
        cnt[0] += 1
        return ks[cnt[0] - 1]

    def nrm(shape, scale):
        return scale * jax.random.normal(nk(), shape, F32)

    def gain(shape):
        return 1.0 + nrm(shape, 0.02)

    dt0 = jnp.exp(jax.random.uniform(nk(), (N_A_LAYERS, SSM_HEADS), F32, np.log(1e-3), np.log(1e-1)))
    d = {
        'x_prompt': nrm((BATCH, SEQ, D_MODEL), 1.0),
        'x_sample': nrm((DEC_BATCH, DEC_SEQ, D_MODEL), 1.0),
        'state_ssm': nrm((N_A_LAYERS, DEC_BATCH, SSM_HEADS, SSM_HEAD_DIM, D_STATE), 0.5),
        'state_conv': nrm((N_A_LAYERS, DEC_BATCH, CONV_W - 1, CONV_DIM), 1.0),
        'cache_k': nrm((DEC_BATCH, WINDOW, ATT_KV_HEADS, ATT_HEAD_DIM), 1.0),
        'cache_v': nrm((DEC_BATCH, WINDOW, ATT_KV_HEADS, ATT_HEAD_DIM), 1.0),
        'p_prompt': nrm((DEPTH, BATCH, SEQ, D_PLE), 1.0),
        'p_sample': nrm((DEPTH, DEC_BATCH, DEC_SEQ, D_PLE), 1.0),
        'ffn1_norm': gain((DEPTH, D_MODEL)),
        'ffn1_wi': nrm((DEPTH, D_MODEL, 2 * D_FF), D_MODEL ** -0.5),
        'ffn1_wo': nrm((DEPTH, D_FF, D_MODEL), D_FF ** -0.5),
        'mix_norm': gain((DEPTH, D_MODEL)),
        'ffn2_norm': gain((DEPTH, D_MODEL)),
        'ffn2_wi': nrm((DEPTH, D_MODEL, 2 * D_FF), D_MODEL ** -0.5),
        'ffn2_wo': nrm((DEPTH, D_FF, D_MODEL), D_FF ** -0.5),
        'ple_norm': gain((DEPTH, D_MODEL)),
        'ple_gate': nrm((DEPTH, D_MODEL, D_MODEL), D_MODEL ** -0.5),
        'ple_proj': nrm((DEPTH, D_PLE, D_MODEL), D_PLE ** -0.5),
        'ssm_in': nrm((N_A_LAYERS, D_MODEL, D_INNER + CONV_DIM + SSM_HEADS), D_MODEL ** -0.5),
        'ssm_conv_w': nrm((N_A_LAYERS, CONV_W, CONV_DIM), CONV_W ** -0.5),
        'ssm_conv_b': nrm((N_A_LAYERS, CONV_DIM), 0.01),
        'ssm_dt_bias': dt0 + jnp.log(-jnp.expm1(-dt0)),
        'ssm_a_log': jnp.log(jax.random.uniform(nk(), (N_A_LAYERS, SSM_HEADS), F32, 1.0, 16.0)),
        'ssm_d': gain((N_A_LAYERS, SSM_HEADS)),
        'ssm_norm': gain((N_A_LAYERS, D_INNER)),
        'ssm_out': nrm((N_A_LAYERS, D_INNER, D_MODEL), D_INNER ** -0.5),
        'kv_norm': gain((D_MODEL,)),
        'w_kv': nrm((D_MODEL, 2 * ATT_KV_HEADS * ATT_HEAD_DIM), D_MODEL ** -0.5),
        'k_norm': gain((ATT_HEAD_DIM,)),
        'w_q': nrm((N_B_LAYERS, D_MODEL, ATT_HEADS * ATT_HEAD_DIM), D_MODEL ** -0.5),
        'q_norm': gain((N_B_LAYERS, ATT_HEAD_DIM)),
        'attn_sinks': nrm((N_B_LAYERS, ATT_HEADS), 1.0),
        'w_o': nrm((N_B_LAYERS, ATT_HEADS * ATT_HEAD_DIM, D_MODEL), (ATT_HEADS * ATT_HEAD_DIM) ** -0.5),
    }
    return d


def reference(x_prompt, x_sample, state_ssm, state_conv, cache_k, cache_v, p_prompt, p_sample,
              ffn1_norm, ffn1_wi, ffn1_wo, mix_norm, ffn2_norm, ffn2_wi, ffn2_wo,
              ple_norm, ple_gate, ple_proj,
              ssm_in, ssm_conv_w, ssm_conv_b, ssm_dt_bias, ssm_a_log, ssm_d, ssm_norm, ssm_out,
              kv_norm, w_kv, k_norm, w_q, q_norm, attn_sinks, w_o):
    w = {'ffn1_norm': ffn1_norm, 'ffn1_wi': ffn1_wi, 'ffn1_wo': ffn1_wo, 'mix_norm': mix_norm,
         'ffn2_norm': ffn2_norm, 'ffn2_wi': ffn2_wi, 'ffn2_wo': ffn2_wo,
         'ple_norm': ple_norm, 'ple_gate': ple_gate, 'ple_proj': ple_proj,
         'ssm_in': ssm_in, 'ssm_conv_w': ssm_conv_w, 'ssm_conv_b': ssm_conv_b, 'ssm_dt_bias': ssm_dt_bias,
         'ssm_a_log': ssm_a_log, 'ssm_d': ssm_d, 'ssm_norm': ssm_norm, 'ssm_out': ssm_out,
         'kv_norm': kv_norm, 'w_kv': w_kv, 'k_norm': k_norm, 'w_q': w_q, 'q_norm': q_norm,
         'attn_sinks': attn_sinks, 'w_o': w_o}
    bp = x_prompt.shape[0]
    ssm0_p = jnp.zeros((N_A_LAYERS, bp, SSM_HEADS, SSM_HEAD_DIM, D_STATE), x_prompt.dtype)
    conv0_p = jnp.zeros((N_A_LAYERS, bp, CONV_W - 1, CONV_DIM), x_prompt.dtype)
    kv0_p = jnp.zeros((bp, WINDOW, ATT_KV_HEADS, ATT_HEAD_DIM), x_prompt.dtype)
    y_prompt, ssm_p, conv_p, k_p, v_p = trunk(x_prompt, p_prompt, ssm0_p, conv0_p, kv0_p, kv0_p, 0, False, w)
    y_sample, ssm_s, conv_s, k_s, v_s = trunk(x_sample, p_sample, state_ssm, state_conv, cache_k, cache_v, PAST_LEN, True, w)
    return (y_prompt, y_sample, ssm_p, conv_p, k_p, v_p, ssm_s, conv_s, k_s, v_s)
```

```python
import functools

import numpy as np
import jax
import jax.numpy as jnp
from jax import lax
from jax.experimental import pallas as pl
from jax.experimental.pallas import tpu as pltpu

F32 = jnp.float32
BF16 = jnp.bfloat16
EPS = 1e-6
PAST_LEN = 8192
ROPE_THETA = 10000.0
LANES = 128
SSD_CHUNK = 128
VMEM_BYTES = 56 * 1024 * 1024


def _dot(a, b):
    return jnp.dot(a, b, preferred_element_type=F32)


def _dot_nt(a, b):
    return lax.dot_general(a, b, (((1,), (1,)), ((), ())), preferred_element_type=F32)


def _split3(a):
    hi = a.astype(BF16)
    r1 = a - hi.astype(F32)
    mid = r1.astype(BF16)
    lo = (r1 - mid.astype(F32)).astype(BF16)
    return hi, mid, lo


def _dot_split_lhs(a, b_exact, parts=2):
    hi, mid, lo = _split3(a)
    out = _dot(hi, b_exact) + _dot(mid, b_exact)
    if parts == 3:
        out = out + _dot(lo, b_exact)
    return out


def _dot_split_rhs(a_exact, b, parts=2):
    hi, mid, lo = _split3(b)
    out = _dot(a_exact, hi) + _dot(a_exact, mid)
    if parts == 3:
        out = out + _dot(a_exact, lo)
    return out


def _rms(x, g):
    ms = jnp.mean(x * x, axis=-1, keepdims=True)
    return x * lax.rsqrt(ms + EPS) * g


def _silu(x):
    return x * jax.nn.sigmoid(x)


def _softplus(v):
    return jnp.maximum(v, 0.0) + jnp.log1p(jnp.exp(-jnp.abs(v)))


def _seg_rms(x, seg_sum, seg_exp, seg):
    ms = _dot_split_lhs(x * x, seg_sum) * (1.0 / seg)
    return _dot_split_lhs(lax.rsqrt(ms + EPS), seg_exp)


def _rope(x, cos, sin_signed, half):
    width = x.shape[-1]
    lane = lax.broadcasted_iota(jnp.int32, x.shape, 1)
    first = (lane % (2 * half)) < half
    rot = jnp.where(first, pltpu.roll(x, width - half, 1), pltpu.roll(x, half, 1))
    return x * cos + rot * sin_signed


def _const_spec(shape):
    zeros = (0,) * len(shape)
    return pl.BlockSpec(shape, lambda *_: zeros, pipeline_mode=pl.Buffered(1))


def _params(sem):
    return pltpu.CompilerParams(dimension_semantics=sem, vmem_limit_bytes=VMEM_BYTES)


def _seg_mats(width, seg):
    s = np.zeros((width, LANES), np.float32)
    s[np.arange(width), np.arange(width) // seg] = 1.0
    return jnp.asarray(s, BF16), jnp.asarray(s.T.copy(), BF16)


def _ffn_kernel(x_ref, g_ref, wi_ref, wo_ref, o_ref, h_scr, *, dff, fc):
    x = x_ref[...]
    xn = _rms(x, g_ref[...]).astype(BF16)
    for c in range(dff // fc):
        gt = _dot(xn, wi_ref[:, c * fc:(c + 1) * fc])
        up = _dot(xn, wi_ref[:, dff + c * fc:dff + (c + 1) * fc])
        h_scr[:, c * fc:(c + 1) * fc] = (_silu(gt) * up).astype(BF16)
    o_ref[...] = x + 0.5 * _dot(h_scr[...], wo_ref[...])


def _ffn(x, g, wi, wo, tm):
    m, d = x.shape
    dff = wo.shape[0]
    fc = 256 if dff % 256 == 0 else dff
    return pl.pallas_call(
        functools.partial(_ffn_kernel, dff=dff, fc=fc),
        grid=(m // tm,),
        in_specs=[pl.BlockSpec((tm, d), lambda i: (i, 0)), _const_spec((1, d)),
                  _const_spec((d, 2 * dff)), _const_spec((dff, d))],
        out_specs=pl.BlockSpec((tm, d), lambda i: (i, 0)),
        out_shape=jax.ShapeDtypeStruct((m, d), F32),
        scratch_shapes=[pltpu.VMEM((tm, dff), BF16)],
        compiler_params=_params(("parallel",)),
        name="ffn",
    )(x, g, wi, wo)


def _ple_kernel(x_ref, g_ref, wg_ref, p_ref, wp_ref, o_ref):
    x = x_ref[...]
    gate = jax.nn.sigmoid(_dot(_rms(x, g_ref[...]).astype(BF16), wg_ref[...]))
    o_ref[...] = x + gate * _dot(p_ref[...].astype(BF16), wp_ref[...])


def _ple(x, g, wg, p, wp, tm):
    m, d = x.shape
    dp = p.shape[1]
    return pl.pallas_call(
        _ple_kernel,
        grid=(m // tm,),
        in_specs=[pl.BlockSpec((tm, d), lambda i: (i, 0)), _const_spec((1, d)), _const_spec((d, d)),
                  pl.BlockSpec((tm, dp), lambda i: (i, 0)), _const_spec((dp, d))],
        out_specs=pl.BlockSpec((tm, d), lambda i: (i, 0)),
        out_shape=jax.ShapeDtypeStruct((m, d), F32),
        compiler_params=_params(("parallel",)),
        name="ple",
    )(x, g, wg, p, wp)


def _mamba_in_kernel(x_ref, g_ref, wz_ref, wx_ref, wd_ref, z_ref, xbc_ref, dt_ref, *, nc):
    xn = _rms(x_ref[...], g_ref[...]).astype(BF16)
    z_ref[...] = _dot(xn, wz_ref[...])
    cw = wx_ref.shape[1] // nc
    for c in range(nc):
        xbc_ref[:, c * cw:(c + 1) * cw] = _dot(xn, wx_ref[:, c * cw:(c + 1) * cw])
    dt_ref[...] = _dot(xn, wd_ref[...])


def _mamba_in(x, g, wz, wx, wd, tm):
    m, d = x.shape
    di, cd = wz.shape[1], wx.shape[1]
    row = lambda w: pl.BlockSpec((tm, w), lambda i: (i, 0))
    return pl.pallas_call(
        functools.partial(_mamba_in_kernel, nc=4),
        grid=(m // tm,),
        in_specs=[row(d), _const_spec((1, d)), _const_spec((d, di)), _const_spec((d, cd)),
                  _const_spec((d, LANES))],
        out_specs=[row(di), row(cd), row(LANES)],
        out_shape=[jax.ShapeDtypeStruct((m, di), F32), jax.ShapeDtypeStruct((m, cd), F32),
                   jax.ShapeDtypeStruct((m, LANES), F32)],
        compiler_params=_params(("parallel",)),
        name="mamba_in",
    )(x, g, wz, wx, wd)


def _mamba_out_kernel(x_ref, y_ref, z_ref, ng_ref, ssum_ref, sexp_ref, wo_ref, o_ref, *, gw):
    gated = y_ref[...] * _silu(z_ref[...])
    r = _seg_rms(gated, ssum_ref[...], sexp_ref[...], gw)
    o_ref[...] = x_ref[...] + _dot((gated * r * ng_ref[...]).astype(BF16), wo_ref[...])


def _mamba_out(x, y, z, ng, wo, groups, tm):
    m, d = x.shape
    di = y.shape[1]
    gw = di // groups
    ssum, sexp = _seg_mats(di, gw)
    row = lambda w: pl.BlockSpec((tm, w), lambda i: (i, 0))
    return pl.pallas_call(
        functools.partial(_mamba_out_kernel, gw=gw),
        grid=(m // tm,),
        in_specs=[row(d), row(di), row(di), _const_spec((1, di)), _const_spec((di, LANES)),
                  _const_spec((LANES, di)), _const_spec((di, d))],
        out_specs=row(d),
        out_shape=jax.ShapeDtypeStruct((m, d), F32),
        compiler_params=_params(("parallel",)),
        name="mamba_out",
    )(x, y, z, ng, ssum, sexp, wo)


def _qproj_kernel(x_ref, g_ref, wq_ref, qg_ref, ssum_ref, sexp_ref, cos_ref, sin_ref, q_ref, *, hd, scale):
    xn = _rms(x_ref[...], g_ref[...]).astype(BF16)
    q = _dot(xn, wq_ref[...])
    qn = q * _seg_rms(q, ssum_ref[...], sexp_ref[...], hd) * qg_ref[...]
    reps = q.shape[1] // LANES
    cos = jnp.tile(cos_ref[...], (1, reps))
    sin = jnp.tile(sin_ref[...], (1, reps))
    q_ref[...] = _rope(qn, cos, sin, hd // 2) * scale


def _qproj(x, g, wq, qg, cos, sin, hd, tm):
    m, d = x.shape
    dq = wq.shape[1]
    ssum, sexp = _seg_mats(dq, hd)
    nt = cos.shape[0] // tm
    row = lambda w: pl.BlockSpec((tm, w), lambda i: (i, 0))
    tab = pl.BlockSpec((tm, LANES), lambda i: (i % nt, 0))
    return pl.pallas_call(
        functools.partial(_qproj_kernel, hd=hd, scale=hd ** -0.5),
        grid=(m // tm,),
        in_specs=[row(d), _const_spec((1, d)), _const_spec((d, dq)), _const_spec((1, dq)),
                  _const_spec((dq, LANES)), _const_spec((LANES, dq)), tab, tab],
        out_specs=row(dq),
        out_shape=jax.ShapeDtypeStruct((m, dq), F32),
        compiler_params=_params(("parallel",)),
        name="qproj",
    )(x, g, wq, qg, ssum, sexp, cos, sin)


def _kv_kernel(x_ref, g_ref, wk_ref, wv_ref, kg_ref, ssum_ref, sexp_ref, cos_ref, sin_ref, k_ref, v_ref, *, hd):
    xn = _rms(x_ref[...], g_ref[...]).astype(BF16)
    k = _dot(xn, wk_ref[...])
    v_ref[...] = _dot(xn, wv_ref[...])
    kn = k * _seg_rms(k, ssum_ref[...], sexp_ref[...], hd) * kg_ref[...]
    reps = k.shape[1] // LANES
    k_ref[...] = _rope(kn, jnp.tile(cos_ref[...], (1, reps)), jnp.tile(sin_ref[...], (1, reps)), hd // 2)


def _kv(x, g, wk, wv, kg, cos, sin, hd, tm):
    m, d = x.shape
    dk = wk.shape[1]
    ssum, sexp = _seg_mats(dk, hd)
    nt = cos.shape[0] // tm
    row = lambda w: pl.BlockSpec((tm, w), lambda i: (i, 0))
    tab = pl.BlockSpec((tm, LANES), lambda i: (i % nt, 0))
    return pl.pallas_call(
        functools.partial(_kv_kernel, hd=hd),
        grid=(m // tm,),
        in_specs=[row(d), _const_spec((1, d)), _const_spec((d, dk)), _const_spec((d, dk)), _const_spec((1, dk)),
                  _const_spec((dk, LANES)), _const_spec((LANES, dk)), tab, tab],
        out_specs=[row(dk), row(dk)],
        out_shape=[jax.ShapeDtypeStruct((m, dk), F32), jax.ShapeDtypeStruct((m, dk), F32)],
        compiler_params=_params(("parallel",)),
        name="kv",
    )(x, g, wk, wv, kg, ssum, sexp, cos, sin)


def _attn_out_kernel(x_ref, o_ref_in, wo_ref, o_ref):
    o_ref[...] = x_ref[...] + _dot(o_ref_in[...].astype(BF16), wo_ref[...])


def _attn_out(x, o, wo, tm):
    m, d = x.shape
    dq = o.shape[1]
    row = lambda w: pl.BlockSpec((tm, w), lambda i: (i, 0))
    return pl.pallas_call(
        _attn_out_kernel,
        grid=(m // tm,),
        in_specs=[row(d), row(dq), _const_spec((dq, d))],
        out_specs=row(d),
        out_shape=jax.ShapeDtypeStruct((m, d), F32),
        compiler_params=_params(("parallel",)),
        name="attn_out",
    )(x, o, wo)


def _ssd_kernel(xbc_ref, dt_ref, cw_ref, cb_ref, dtb_ref, alog_ref, dsk_ref, tril_ref, e_ref, et_ref,
                y_ref, h_ref, cbuf, *, tt, L, G, EH, P, N):
    t = pl.program_id(1)
    cd = cbuf.shape[1]
    di = G * EH * P
    gw = EH * P

    @pl.when(t == 0)
    def _():
        h_ref[...] = jnp.zeros_like(h_ref)
        cbuf[0:8, :] = jnp.zeros((8, cd), F32)

    cbuf[8:8 + tt, :] = xbc_ref[...]
    a = -jnp.exp(alog_ref[...])
    tril = tril_ref[...]
    emat = e_ref[...]
    rows = lax.broadcasted_iota(jnp.int32, (L, L), 0)
    cols = lax.broadcasted_iota(jnp.int32, (L, L), 1)
    causal = rows >= cols
    lane = lax.broadcasted_iota(jnp.int32, (L, gw), 1)
    head_masks = [(lane // P) == e for e in range(EH)]

    for c in range(tt // L):
        base = 8 + c * L
        conv = cb_ref[...]
        for k in range(4):
            conv = conv + cw_ref[k:k + 1, :] * cbuf[base - 3 + k:base - 3 + k + L, :]
        act = _silu(conv)
        xs = act[:, :di]
        bm = act[:, di:di + G * N]
        cm = act[:, di + G * N:]
        dt = _softplus(dt_ref[c * L:(c + 1) * L, :] + dtb_ref[...])
        cum = _dot_split_rhs(tril, dt * a, parts=3)
        cum_t = cum.T
        to_end = jnp.exp(cum[L - 1:L, :] - cum)
        dt_e = _dot_split_lhs(dt, emat)
        dte_e = _dot_split_lhs(dt * to_end, emat)
        ec_e = _dot_split_lhs(jnp.exp(cum), emat)
        xdt = (xs * dt_e).astype(BF16)
        xw_t = (xs * dte_e).T.astype(BF16)
        chunk_decay = jnp.broadcast_to(jnp.exp(cum_t[:, L - 1:L]), (LANES, N))
        drow = _dot_split_rhs(et_ref[...], chunk_decay)

        for g in range(G):
            bg = bm[:, g * N:(g + 1) * N].astype(BF16)
            cg = cm[:, g * N:(g + 1) * N].astype(BF16)
            cb = _dot_nt(cg, bg)
            sl = slice(g * gw, (g + 1) * gw)
            xdt_g = xdt[:, sl]
            yg = None
            for e in range(EH):
                h = g * EH + e
                seg = cum[:, h:h + 1] - cum_t[h:h + 1, :]
                decay = jnp.exp(jnp.where(causal, seg, -jnp.inf))
                part = _dot((cb * decay).astype(BF16), jnp.where(head_masks[e], xdt_g, jnp.zeros_like(xdt_g)))
                yg = part if yg is None else yg + part
            hg = h_ref[0, sl, :]
            y_inter = _dot_nt(cg, hg.astype(BF16)) * ec_e[:, sl]
            y_ref[c * L:(c + 1) * L, sl] = yg + y_inter + dsk_ref[:, sl] * xs[:, sl]
            h_ref[0, sl, :] = hg * drow[sl, :] + _dot(xw_t[sl, :], bg)

    cbuf[0:8, :] = cbuf[tt:tt + 8, :]


def _ssd(xbc, dt_raw, cw, cb, dtb, alog, dsk, bsz, G, EH, P, N, tt):
    m, cd = xbc.shape
    t_len = m // bsz
    nt = t_len // tt
    L = min(SSD_CHUNK, tt)
    di = G * EH * P
    heads = G * EH
    tril = jnp.asarray(np.tril(np.ones((L, L), np.float32)), BF16)
    emat_np = np.zeros((LANES, di), np.float32)
    emat_np[np.arange(di) // P, np.arange(di)] = 1.0
    emat = jnp.asarray(emat_np, BF16)
    emat_t = jnp.asarray(emat_np.T.copy(), BF16)
    row = lambda w: pl.BlockSpec((tt, w), lambda b, t: (b * nt + t, 0))
    y, h = pl.pallas_call(
        functools.partial(_ssd_kernel, tt=tt, L=L, G=G, EH=EH, P=P, N=N),
        grid=(bsz, nt),
        in_specs=[row(cd), row(LANES), _const_spec((4, cd)), _const_spec((1, cd)), _const_spec((1, LANES)),
                  _const_spec((1, LANES)), _const_spec((1, di)), _const_spec((L, L)),
                  _const_spec((LANES, di)), _const_spec((di, LANES))],
        out_specs=[row(di), pl.BlockSpec((1, heads * P, N), lambda b, t: (b, 0, 0))],
        out_shape=[jax.ShapeDtypeStruct((m, di), F32), jax.ShapeDtypeStruct((bsz, heads * P, N), F32)],
        scratch_shapes=[pltpu.VMEM((tt + 8, cd), F32)],
        compiler_params=_params(("parallel", "arbitrary")),
        name="ssd",
    )(xbc, dt_raw, cw, cb, dtb, alog, dsk, tril, emat, emat_t)
    return y, h


def _ssm_step_kernel(h0_ref, xbc_ref, c0_ref, dt_ref, cw_ref, cb_ref, dtb_ref, alog_ref, dsk_ref, e_ref, et_ref,
                     hn_ref, y_ref, cnew_ref, src_hi, src_mid, src_lo, b_scr, c_scr, xs_scr,
                     *, bb, G, EH, P, N):
    i = pl.program_id(0)
    cd = xbc_ref.shape[1]
    di = G * EH * P
    gw = EH * P
    db = xbc_ref.shape[0]

    @pl.when(i == 0)
    def _():
        x = xbc_ref[...]
        conv = cb_ref[...] + cw_ref[3:4, :] * x
        for k in range(3):
            conv = conv + cw_ref[k:k + 1, :] * c0_ref[:, k * cd:(k + 1) * cd]
        cnew_ref[:, 0:2 * cd] = c0_ref[:, cd:3 * cd]
        cnew_ref[:, 2 * cd:3 * cd] = x
        act = _silu(conv)
        xs = act[:, :di]
        xs_scr[...] = xs
        b_scr[...] = act[:, di:di + G * N]
        c_scr[...] = act[:, di + G * N:]
        dt = _softplus(dt_ref[...] + dtb_ref[...])
        decay = jnp.exp(dt * (-jnp.exp(alog_ref[...])))
        xdt_t = (xs * _dot_split_lhs(dt, e_ref[...], parts=3)).T
        decay_t = _dot_split_rhs(et_ref[...], decay.T, parts=3)
        for part, src in zip(_split3(jnp.concatenate([xdt_t, decay_t], axis=0)), (src_hi, src_mid, src_lo)):
            src[...] = part

    for j in range(bb):
        b = i * bb + j
        onehot = jnp.where(lax.broadcasted_iota(jnp.int32, (db, LANES), 0) == b, 1.0, 0.0).astype(BF16)
        col = _dot(src_hi[...], onehot) + _dot(src_mid[...], onehot) + _dot(src_lo[...], onehot)
        brow = b_scr[pl.ds(b, 1), :]
        crow = c_scr[pl.ds(b, 1), :]
        parts = []
        for g in range(G):
            sl = slice(g * gw, (g + 1) * gw)
            hn = h0_ref[j, sl, :] * col[di + g * gw:di + (g + 1) * gw, :] + col[sl, :] * brow[:, g * N:(g + 1) * N]
            hn_ref[j, sl, :] = hn
            c8 = jnp.broadcast_to(crow[:, g * N:(g + 1) * N], (8, N)).astype(BF16)
            parts.append(_dot_nt(c8, hn.astype(BF16))[0:1, :])
        y_ref[pl.ds(b, 1), :] = jnp.concatenate(parts, axis=1) + dsk_ref[...] * xs_scr[pl.ds(b, 1), :]


def _ssm_step(h0, xbc, c0, dt_raw, cw, cb, dtb, alog, dsk, G, EH, P, N, bb):
    db, cd = xbc.shape
    di = G * EH * P
    hp = G * EH * P
    emat_np = np.zeros((LANES, di), np.float32)
    emat_np[np.arange(di) // P, np.arange(di)] = 1.0
    emat = jnp.asarray(emat_np, BF16)
    emat_t = jnp.asarray(emat_np.T.copy(), BF16)
    st = pl.BlockSpec((bb, hp, N), lambda i: (i, 0, 0))
    return pl.pallas_call(
        functools.partial(_ssm_step_kernel, bb=bb, G=G, EH=EH, P=P, N=N),
        grid=(db // bb,),
        in_specs=[st, _const_spec((db, cd)), _const_spec((db, 3 * cd)), _const_spec((db, LANES)),
                  _const_spec((4, cd)), _const_spec((1, cd)), _const_spec((1, LANES)), _const_spec((1, LANES)),
                  _const_spec((1, di)), _const_spec((LANES, di)), _const_spec((di, LANES))],
        out_specs=[st, pl.BlockSpec((db, di), lambda i: (0, 0)), pl.BlockSpec((db, 3 * cd), lambda i: (0, 0))],
        out_shape=[jax.ShapeDtypeStruct((db, hp, N), F32), jax.ShapeDtypeStruct((db, di), F32),
                   jax.ShapeDtypeStruct((db, 3 * cd), F32)],
        scratch_shapes=[pltpu.VMEM((2 * hp, db), BF16)] * 3
                       + [pltpu.VMEM((db, G * N), F32), pltpu.VMEM((db, G * N), F32), pltpu.VMEM((db, di), F32)],
        compiler_params=_params(("arbitrary",)),
        name="ssm_step",
    )(h0, xbc, c0, dt_raw, cw, cb, dtb, alog, dsk, emat, emat_t)


def _attn_kernel(q_ref, ka_ref, kb_ref, va_ref, vb_ref, sink_ref, o_ref, *, tq, W, KV, GQ, hd):
    t = pl.program_id(1)
    r = lax.broadcasted_iota(jnp.int32, (GQ * W, W), 0) % W
    c = lax.broadcasted_iota(jnp.int32, (GQ * W, W), 1)
    mask_prev = c >= r
    mask_cur = c <= r
    for j in range(tq // W):
        qb = q_ref[j * W:(j + 1) * W, :].astype(BF16)
        if j + 1 < tq // W:
            k_prev, v_prev = ka_ref[0, j * W:(j + 1) * W, :], va_ref[0, j * W:(j + 1) * W, :]
            k_cur, v_cur = ka_ref[0, (j + 1) * W:(j + 2) * W, :], va_ref[0, (j + 1) * W:(j + 2) * W, :]
        else:
            k_prev, v_prev = ka_ref[0, j * W:(j + 1) * W, :], va_ref[0, j * W:(j + 1) * W, :]
            k_cur, v_cur = kb_ref[0], vb_ref[0]
        prev_valid = jnp.logical_or(t > 0, j > 0)
        outs = [None] * (KV * GQ)
        for kh in range(KV):
            ks = slice(kh * hd, (kh + 1) * hd)
            qg = jnp.concatenate([qb[:, (e * KV + kh) * hd:(e * KV + kh + 1) * hd] for e in range(GQ)], axis=0)
            sink = jnp.concatenate([jnp.broadcast_to(sink_ref[:, e * KV + kh:e * KV + kh + 1], (W, 1))
                                    for e in range(GQ)], axis=0)
            s_prev = jnp.where(jnp.logical_and(mask_prev, prev_valid), _dot_nt(qg, k_prev[:, ks].astype(BF16)), -jnp.inf)
            s_cur = jnp.where(mask_cur, _dot_nt(qg, k_cur[:, ks].astype(BF16)), -jnp.inf)
            mx = jnp.maximum(jnp.maximum(jnp.max(s_prev, axis=-1, keepdims=True),
                                         jnp.max(s_cur, axis=-1, keepdims=True)), sink)
            p_prev = jnp.exp(s_prev - mx)
            p_cur = jnp.exp(s_cur - mx)
            den = jnp.sum(p_prev, axis=-1, keepdims=True) + jnp.sum(p_cur, axis=-1, keepdims=True) + jnp.exp(sink - mx)
            og = (_dot(p_prev.astype(BF16), v_prev[:, ks].astype(BF16))
                  + _dot(p_cur.astype(BF16), v_cur[:, ks].astype(BF16))) / den
            for e in range(GQ):
                outs[e * KV + kh] = og[e * W:(e + 1) * W, :]
        o_ref[j * W:(j + 1) * W, :] = jnp.concatenate(outs, axis=1)


def _attn(q, k_all, v_all, sinks, bsz, W, KV, GQ, hd, tq):
    m, dq = q.shape
    t_len = m // bsz
    nt = t_len // tq
    dk = KV * hd
    qrow = pl.BlockSpec((tq, dq), lambda b, t: (b * nt + t, 0))
    main = pl.BlockSpec((1, tq, dk), lambda b, t: (b, t, 0))
    tail = pl.BlockSpec((1, W, dk), lambda b, t: (b, (t + 1) * (tq // W), 0))
    return pl.pallas_call(
        functools.partial(_attn_kernel, tq=tq, W=W, KV=KV, GQ=GQ, hd=hd),
        grid=(bsz, nt),
        in_specs=[qrow, main, tail, main, tail, _const_spec((1, LANES))],
        out_specs=qrow,
        out_shape=jax.ShapeDtypeStruct((m, dq), F32),
        compiler_params=_params(("parallel", "parallel")),
        name="attn",
    )(q, k_all, k_all, v_all, v_all, sinks)


def _attn_step_kernel(q_ref, kc_ref, vc_ref, kn_ref, vn_ref, sink_ref, o_ref, *, bb, KV, GQ, hd):
    i = pl.program_id(0)
    nh = KV * GQ
    dq = nh * hd
    dk = KV * hd
    lane_q = lax.broadcasted_iota(jnp.int32, (nh, dq), 1)
    row_q = lax.broadcasted_iota(jnp.int32, (nh, dq), 0)
    own_q = (lane_q // hd) == row_q
    lane_k = lax.broadcasted_iota(jnp.int32, (nh, dk), 1)
    row_k = lax.broadcasted_iota(jnp.int32, (nh, dk), 0)
    own_k = (lane_k // hd) == (row_k % KV)
    sink = sink_ref[...]
    for j in range(bb):
        b = i * bb + j
        qm = jnp.where(own_q, jnp.broadcast_to(q_ref[pl.ds(b, 1), :], (nh, dq)), 0.0)
        qbd = qm[:, 0:dk]
        for e in range(1, GQ):
            qbd = qbd + qm[:, e * dk:(e + 1) * dk]
        s = _dot_nt(qbd.astype(BF16), kc_ref[j].astype(BF16))
        s_new = jnp.sum(qbd * kn_ref[pl.ds(b, 1), :], axis=-1, keepdims=True)
        mx = jnp.maximum(jnp.maximum(jnp.max(s, axis=-1, keepdims=True), s_new), sink)
        p = jnp.exp(s - mx)
        p_new = jnp.exp(s_new - mx)
        den = jnp.sum(p, axis=-1, keepdims=True) + p_new + jnp.exp(sink - mx)
        o = (_dot(p.astype(BF16), vc_ref[j].astype(BF16)) + p_new * vn_ref[pl.ds(b, 1), :]) / den
        om = jnp.where(own_k, o, 0.0)
        pieces = [jnp.sum(om[e * KV:(e + 1) * KV, :], axis=0, keepdims=True) for e in range(GQ)]
        o_ref[pl.ds(b, 1), :] = jnp.concatenate(pieces, axis=1)


def _attn_step(q, kc, vc, kn, vn, sinks_col, KV, GQ, hd, bb):
    db, dq = q.shape
    W, dk = kc.shape[1], kc.shape[2]
    nh = KV * GQ
    cache = pl.BlockSpec((bb, W, dk), lambda i: (i, 0, 0))
    return pl.pallas_call(
        functools.partial(_attn_step_kernel, bb=bb, KV=KV, GQ=GQ, hd=hd),
        grid=(db // bb,),
        in_specs=[_const_spec((db, dq)), cache, cache, _const_spec((db, dk)), _const_spec((db, dk)),
                  _const_spec((nh, 1))],
        out_specs=pl.BlockSpec((db, dq), lambda i: (0, 0)),
        out_shape=jax.ShapeDtypeStruct((db, dq), F32),
        compiler_params=_params(("arbitrary",)),
        name="attn_step",
    )(q, kc, vc, kn, vn, sinks_col)


def _rope_tables(pos, hd):
    half = hd // 2
    inv = ROPE_THETA ** (-jnp.arange(half, dtype=F32) / half)
    ang = pos.astype(F32)[:, None] * inv[None, :]
    cos, sin = jnp.cos(ang), jnp.sin(ang)
    reps = LANES // hd
    return (jnp.tile(jnp.concatenate([cos, cos], axis=1), (1, reps)),
            jnp.tile(jnp.concatenate([-sin, sin], axis=1), (1, reps)))


def _pad_lanes(v):
    return jnp.pad(v, ((0, 0), (0, LANES - v.shape[1])))


def _trunk(x, p, ssm0, conv0, k_hist, v_hist, pos0, w, dims, is_prompt):
    bsz, t_len, d = x.shape
    m = bsz * t_len
    depth, n_a, G, EH, P, N, KV, GQ, hd, W = dims
    di = G * EH * P
    tm = 512 if m % 512 == 0 else m
    x = x.reshape(m, d)
    cos, sin = _rope_tables(pos0 + jnp.arange(t_len if is_prompt else m) * (1 if is_prompt else 0), hd)
    new_ssm, new_conv = [], []
    k_all = v_all = k_new = v_new = None
    for i in range(depth):
        x = _ffn(x, w['ffn1_norm'][i], w['ffn1_wi'][i], w['ffn1_wo'][i], tm)
        if i < n_a:
            z, xbc, dt_raw = _mamba_in(x, w['mix_norm'][i], w['ssm_wz'][i], w['ssm_wx'][i], w['ssm_wd'][i], tm)
            args = (w['ssm_conv_w'][i], w['ssm_conv_b'][i], w['ssm_dt_bias'][i], w['ssm_a_log'][i], w['ssm_dsk'][i])
            cd = xbc.shape[1]
            if is_prompt:
                y, h_last = _ssd(xbc, dt_raw, *args, bsz, G, EH, P, N, tt=min(256, t_len))
                c_last = xbc.reshape(bsz, t_len, cd)[:, t_len - 3:]
            else:
                h_last, y, c_last = _ssm_step(ssm0[i].reshape(bsz, G * EH * P, N), xbc, conv0[i].reshape(bsz, 3 * cd),
                                              dt_raw, *args, G, EH, P, N, bb=4)
                c_last = c_last.reshape(bsz, 3, cd)
            new_ssm.append(h_last.reshape(bsz, G * EH, P, N))
            new_conv.append(c_last)
            x = _mamba_out(x, y, z, w['ssm_norm'][i], w['ssm_out'][i], G, tm)
        else:
            j = i - n_a
            q = _qproj(x, w['mix_norm'][i], w['w_q'][j], w['q_norm'][j], cos, sin, hd, tm)
            if is_prompt:
                o = _attn(q, k_all, v_all, w['sinks_row'][j], bsz, W, KV, GQ, hd, tq=min(256, t_len))
            else:
                o = _attn_step(q, k_hist, v_hist, k_new, v_new, w['sinks_col'][j], KV, GQ, hd, bb=8)
            x = _attn_out(x, o, w['w_o'][j], tm)
        x = _ffn(x, w['ffn2_norm'][i], w['ffn2_wi'][i], w['ffn2_wo'][i], tm)
        x = _ple(x, w['ple_norm'][i], w['ple_gate'][i], p[i].reshape(m, -1), w['ple_proj'][i], tm)
        if i == n_a - 1:
            k_new, v_new = _kv(x, w['kv_norm'], w['w_k'], w['w_v'], w['k_norm'], cos, sin, hd, tm)
            dk = KV * hd
            k_all = jnp.concatenate([k_hist, k_new.reshape(bsz, t_len, dk)], axis=1)
            v_all = jnp.concatenate([v_hist, v_new.reshape(bsz, t_len, dk)], axis=1)
    k_out = k_all[:, -W:].reshape(bsz, W, KV, hd)
    v_out = v_all[:, -W:].reshape(bsz, W, KV, hd)
    return x.reshape(bsz, t_len, d), jnp.stack(new_ssm), jnp.stack(new_conv), k_out, v_out


def kernel(x_prompt, x_sample, state_ssm, state_conv, cache_k, cache_v, p_prompt, p_sample, ffn1_norm, ffn1_wi, ffn1_wo, mix_norm, ffn2_norm, ffn2_wi, ffn2_wo, ple_norm, ple_gate, ple_proj, ssm_in, ssm_conv_w, ssm_conv_b, ssm_dt_bias, ssm_a_log, ssm_d, ssm_norm, ssm_out, kv_norm, w_kv, k_norm, w_q, q_norm, attn_sinks, w_o):
    depth, d = ffn1_norm.shape
    n_a, heads = ssm_a_log.shape
    di = ssm_norm.shape[1]
    cd = ssm_conv_b.shape[1]
    N = state_ssm.shape[-1]
    P = state_ssm.shape[-2]
    G = (cd - di) // (2 * N)
    EH = heads // G
    W, KV, hd = cache_k.shape[1], cache_k.shape[2], cache_k.shape[3]
    nh = w_q.shape[2] // hd
    GQ = nh // KV
    dims = (depth, n_a, G, EH, P, N, KV, GQ, hd, W)

    slot_head = np.array([(s % KV) * GQ + s // KV for s in range(nh)])
    col_perm = (slot_head[:, None] * hd + np.arange(hd)[None, :]).reshape(-1)
    row1 = lambda v: v[:, None, :]
    sinks_slot = attn_sinks[:, slot_head]
    dk = KV * hd
    w = {
        'ffn1_norm': row1(ffn1_norm), 'ffn1_wi': ffn1_wi.astype(BF16), 'ffn1_wo': ffn1_wo.astype(BF16),
        'ffn2_norm': row1(ffn2_norm), 'ffn2_wi': ffn2_wi.astype(BF16), 'ffn2_wo': ffn2_wo.astype(BF16),
        'mix_norm': row1(mix_norm), 'ple_norm': row1(ple_norm),
        'ple_gate': ple_gate.astype(BF16), 'ple_proj': ple_proj.astype(BF16),
        'ssm_wz': ssm_in[:, :, :di].astype(BF16), 'ssm_wx': ssm_in[:, :, di:di + cd].astype(BF16),
        'ssm_wd': jnp.pad(ssm_in[:, :, di + cd:], ((0, 0), (0, 0), (0, LANES - heads))).astype(BF16),
        'ssm_conv_w': ssm_conv_w, 'ssm_conv_b': row1(ssm_conv_b),
        'ssm_dt_bias': jnp.pad(ssm_dt_bias, ((0, 0), (0, LANES - heads)))[:, None, :],
        'ssm_a_log': jnp.pad(ssm_a_log, ((0, 0), (0, LANES - heads)))[:, None, :],
        'ssm_dsk': jnp.repeat(ssm_d, P, axis=1)[:, None, :],
        'ssm_norm': row1(ssm_norm), 'ssm_out': ssm_out.astype(BF16),
        'kv_norm': kv_norm[None, :], 'w_k': w_kv[:, :dk].astype(BF16), 'w_v': w_kv[:, dk:].astype(BF16),
        'k_norm': jnp.tile(k_norm, KV)[None, :],
        'w_q': w_q[:, :, col_perm].astype(BF16), 'q_norm': jnp.tile(q_norm, (1, nh))[:, None, :],
        'sinks_row': jnp.pad(sinks_slot, ((0, 0), (0, LANES - nh)))[:, None, :],
        'sinks_col': sinks_slot[:, :, None],
        'w_o': w_o[:, col_perm, :].astype(BF16),
    }
    bp = x_prompt.shape[0]
    bs = x_sample.shape[0]
    kv0 = jnp.zeros((bp, W, dk), F32)
    y_p, ssm_p, conv_p, k_p, v_p = _trunk(x_prompt, p_prompt, None, None, kv0, kv0, 0, w, dims, True)
    y_s, ssm_s, conv_s, k_s, v_s = _trunk(x_sample, p_sample, state_ssm, state_conv,
                                          cache_k.reshape(bs, W, dk), cache_v.reshape(bs, W, dk), PAST_LEN, w, dims, False)
    return (y_p, y_s, ssm_p, conv_p, k_p, v_p, ssm_s, conv_s, k_s, v_s)
```

```python
import functools

import numpy as np
import jax
import jax.numpy as jnp
from jax import lax
from jax.experimental import pallas as pl
from jax.experimental.pallas import tpu as pltpu

F32 = jnp.float32
BF16 = jnp.bfloat16
EPS = 1e-6
PAST_LEN = 8192
ROPE_THETA = 10000.0
LANES = 128
SUBLANES = 8
SSD_CHUNK = 128
CONV_TAPS = 4
VMEM_BYTES = 56 * 1024 * 1024


def _dot(a, b):
    return jnp.dot(a, b, preferred_element_type=F32)


def _dot_nt(a, b):
    return lax.dot_general(a, b, (((1,), (1,)), ((), ())), preferred_element_type=F32)


def _split3(a):
    hi = a.astype(BF16)
    r1 = a - hi.astype(F32)
    mid = r1.astype(BF16)
    lo = (r1 - mid.astype(F32)).astype(BF16)
    return hi, mid, lo


def _dot_split_lhs(a, b_exact, parts=2):
    hi, mid, lo = _split3(a)
    out = _dot(hi, b_exact) + _dot(mid, b_exact)
    if parts == 3:
        out = out + _dot(lo, b_exact)
    return out


def _dot_split_rhs(a_exact, b, parts=2):
    hi, mid, lo = _split3(b)
    out = _dot(a_exact, hi) + _dot(a_exact, mid)
    if parts == 3:
        out = out + _dot(a_exact, lo)
    return out


def _rms(x, g):
    ms = jnp.mean(x * x, axis=-1, keepdims=True)
    return x * lax.rsqrt(ms + EPS) * g


def _silu(x):
    return x * jax.nn.sigmoid(x)


def _softplus(v):
    return jnp.maximum(v, 0.0) + jnp.log1p(jnp.exp(-jnp.abs(v)))


def _seg_rms(x, seg_sum, seg_exp, seg):
    ms = _dot((x * x).astype(BF16), seg_sum) * (1.0 / seg)
    return _dot_split_lhs(lax.rsqrt(ms + EPS), seg_exp)


def _rope(x, cos, sin_signed, half):
    width = x.shape[-1]
    lane = lax.broadcasted_iota(jnp.int32, x.shape, 1)
    first = (lane % (2 * half)) < half
    rot = jnp.where(first, pltpu.roll(x, width - half, 1), pltpu.roll(x, half, 1))
    return x * cos + rot * sin_signed


def _const_spec(shape):
    zeros = (0,) * len(shape)
    return pl.BlockSpec(shape, lambda *_: zeros, pipeline_mode=pl.Buffered(1))


def _params(sem):
    return pltpu.CompilerParams(dimension_semantics=sem, vmem_limit_bytes=VMEM_BYTES)


def _seg_mats(width, seg):
    s = np.zeros((width, LANES), np.float32)
    s[np.arange(width), np.arange(width) // seg] = 1.0
    return jnp.asarray(s, BF16), jnp.asarray(s.T.copy(), BF16)


def _head_expand_mats(heads, p):
    e = np.zeros((LANES, heads * p), np.float32)
    e[np.arange(heads * p) // p, np.arange(heads * p)] = 1.0
    return jnp.asarray(e, BF16), jnp.asarray(e.T.copy(), BF16)


def _half_swiglu(x, g_ref, wi_ref, wo_ref, h_scr, dff, fc):
    xn = _rms(x, g_ref[...]).astype(BF16)
    for c in range(dff // fc):
        gt = _dot(xn, wi_ref[:, c * fc:(c + 1) * fc])
        up = _dot(xn, wi_ref[:, dff + c * fc:dff + (c + 1) * fc])
        h_scr[:, c * fc:(c + 1) * fc] = (_silu(gt) * up).astype(BF16)
    return x + 0.5 * _dot(h_scr[...], wo_ref[...])


def _normed_rope(v, gain_ref, ssum_ref, sexp_ref, cos_ref, sin_ref, hd):
    vn = v * _seg_rms(v, ssum_ref[...], sexp_ref[...], hd) * gain_ref[...]
    reps = v.shape[1] // LANES
    return _rope(vn, jnp.tile(cos_ref[...], (1, reps)), jnp.tile(sin_ref[...], (1, reps)), hd // 2)


def _pre_kernel(*refs, with_q, dff, fc, hd, scale):
    x_ref, g_ref, wi_ref, wo_ref = refs[:4]
    if with_q:
        mg_ref, wq_ref, qg_ref, ssum_ref, sexp_ref, cos_ref, sin_ref, o_ref, q_ref, h_scr = refs[4:]
    else:
        o_ref, h_scr = refs[4:]
    x1 = _half_swiglu(x_ref[...], g_ref, wi_ref, wo_ref, h_scr, dff, fc)
    o_ref[...] = x1
    if with_q:
        q = _dot(_rms(x1, mg_ref[...]).astype(BF16), wq_ref[...])
        q_ref[...] = (_normed_rope(q, qg_ref, ssum_ref, sexp_ref, cos_ref, sin_ref, hd) * scale).astype(q_ref.dtype)


def _pre(x, g, wi, wo, tm, qargs=None, q_dtype=BF16):
    m, d = x.shape
    dff = wo.shape[0]
    fc = 256 if dff % 256 == 0 else dff
    row = lambda w: pl.BlockSpec((tm, w), lambda i: (i, 0))
    in_specs = [row(d), _const_spec((1, d)), _const_spec((d, 2 * dff)), _const_spec((dff, d))]
    args = [x, g, wi, wo]
    out_specs, out_shape = [row(d)], [jax.ShapeDtypeStruct((m, d), F32)]
    hd = 0
    if qargs is not None:
        mg, wq, qg, cos, sin, hd = qargs
        dq = wq.shape[1]
        ssum, sexp = _seg_mats(dq, hd)
        nt = cos.shape[0] // tm
        tab = pl.BlockSpec((tm, LANES), lambda i: (i % nt, 0))
        in_specs += [_const_spec((1, d)), _const_spec((d, dq)), _const_spec((1, dq)),
                     _const_spec((dq, LANES)), _const_spec((LANES, dq)), tab, tab]
        args += [mg, wq, qg, ssum, sexp, cos, sin]
        out_specs.append(row(dq))
        out_shape.append(jax.ShapeDtypeStruct((m, dq), q_dtype))
    return pl.pallas_call(
        functools.partial(_pre_kernel, with_q=qargs is not None, dff=dff, fc=fc, hd=hd, scale=hd ** -0.5 if hd else 1.0),
        grid=(m // tm,),
        in_specs=in_specs, out_specs=out_specs, out_shape=out_shape,
        scratch_shapes=[pltpu.VMEM((tm, dff), BF16)],
        compiler_params=_params(("parallel",)),
        name="pre",
    )(*args)


def _post_kernel(*refs, mamba, with_kv, dff, fc, gw, hd):
    it = iter(refs)
    x_ref = next(it)
    if mamba:
        y_ref, gate_ref, ng_ref, ssum_ref, sexp_ref, wmix_ref = [next(it) for _ in range(6)]
    else:
        att_ref, wmix_ref = next(it), next(it)
    g2_ref, wi_ref, wo_ref, pg_ref, wg_ref, p_ref, wp_ref = [next(it) for _ in range(7)]
    if with_kv:
        kvg_ref, wk_ref, wv_ref, kg_ref, ksum_ref, kexp_ref, cos_ref, sin_ref = [next(it) for _ in range(8)]
    o_ref = next(it)
    if with_kv:
        k_ref, v_ref = next(it), next(it)
    h_scr = next(it)

    if mamba:
        gated = y_ref[...].astype(F32) * gate_ref[...].astype(F32)
        r = _seg_rms(gated, ssum_ref[...], sexp_ref[...], gw)
        mix = _dot((gated * r * ng_ref[...]).astype(BF16), wmix_ref[...])
    else:
        mix = _dot(att_ref[...].astype(BF16), wmix_ref[...])
    x = _half_swiglu(x_ref[...] + mix, g2_ref, wi_ref, wo_ref, h_scr, dff, fc)
    gate = jax.nn.sigmoid(_dot(_rms(x, pg_ref[...]).astype(BF16), wg_ref[...]))
    x = x + gate * _dot(p_ref[...].astype(BF16), wp_ref[...])
    o_ref[...] = x
    if with_kv:
        xn = _rms(x, kvg_ref[...]).astype(BF16)
        v_ref[...] = _dot(xn, wv_ref[...])
        k_ref[...] = _normed_rope(_dot(xn, wk_ref[...]), kg_ref, ksum_ref, kexp_ref, cos_ref, sin_ref, hd)


def _post(x, mix_args, ffn_args, ple_args, tm, kv_args=None):
    m, d = x.shape
    g2, wi, wo = ffn_args
    pg, wg, p, wp = ple_args
    dff = wo.shape[0]
    fc = 256 if dff % 256 == 0 else dff
    dp = p.shape[1]
    row = lambda w: pl.BlockSpec((tm, w), lambda i: (i, 0))
    in_specs, args = [row(d)], [x]
    mamba = len(mix_args) == 5
    gw = hd = 0
    if mamba:
        y, gate, ng, wout, groups = mix_args
        di = y.shape[1]
        gw = di // groups
        ssum, sexp = _seg_mats(di, gw)
        in_specs += [row(di), row(di), _const_spec((1, di)), _const_spec((di, LANES)), _const_spec((LANES, di)),
                     _const_spec((di, d))]
        args += [y, gate, ng, ssum, sexp, wout]
    else:
        att, wmix = mix_args
        in_specs += [row(att.shape[1]), _const_spec(wmix.shape)]
        args += [att, wmix]
    in_specs += [_const_spec((1, d)), _const_spec((d, 2 * dff)), _const_spec((dff, d)),
                 _const_spec((1, d)), _const_spec((d, d)), row(dp), _const_spec((dp, d))]
    args += [g2, wi, wo, pg, wg, p, wp]
    out_specs, out_shape = [row(d)], [jax.ShapeDtypeStruct((m, d), F32)]
    if kv_args is not None:
        kvg, wk, wv, kg, cos, sin, hd = kv_args
        dk = wk.shape[1]
        ksum, kexp = _seg_mats(dk, hd)
        nt = cos.shape[0] // tm
        tab = pl.BlockSpec((tm, LANES), lambda i: (i % nt, 0))
        in_specs += [_const_spec((1, d)), _const_spec((d, dk)), _const_spec((d, dk)), _const_spec((1, dk)),
                     _const_spec((dk, LANES)), _const_spec((LANES, dk)), tab, tab]
        args += [kvg, wk, wv, kg, ksum, kexp, cos, sin]
        out_specs += [row(dk), row(dk)]
        out_shape += [jax.ShapeDtypeStruct((m, dk), F32)] * 2
    return pl.pallas_call(
        functools.partial(_post_kernel, mamba=mamba, with_kv=kv_args is not None, dff=dff, fc=fc, gw=gw, hd=hd),
        grid=(m // tm,),
        in_specs=in_specs, out_specs=out_specs, out_shape=out_shape,
        scratch_shapes=[pltpu.VMEM((tm, dff), BF16)],
        compiler_params=_params(("parallel",)),
        name="post",
    )(*args)


def _mamba_in_kernel(x_ref, g_ref, wz_ref, wx_ref, wd_ref, gate_ref, xbc_ref, dt_ref, *, nc):
    xn = _rms(x_ref[...], g_ref[...]).astype(BF16)
    gate_ref[...] = _silu(_dot(xn, wz_ref[...])).astype(BF16)
    cw = wx_ref.shape[1] // nc
    for c in range(nc):
        xbc_ref[:, c * cw:(c + 1) * cw] = _dot(xn, wx_ref[:, c * cw:(c + 1) * cw])
    dt_ref[...] = _dot(xn, wd_ref[...])


def _mamba_in(x, g, wz, wx, wd, tm):
    m, d = x.shape
    di, cd = wz.shape[1], wx.shape[1]
    row = lambda w: pl.BlockSpec((tm, w), lambda i: (i, 0))
    return pl.pallas_call(
        functools.partial(_mamba_in_kernel, nc=4),
        grid=(m // tm,),
        in_specs=[row(d), _const_spec((1, d)), _const_spec((d, di)), _const_spec((d, cd)),
                  _const_spec((d, LANES))],
        out_specs=[row(di), row(cd), row(LANES)],
        out_shape=[jax.ShapeDtypeStruct((m, di), BF16), jax.ShapeDtypeStruct((m, cd), F32),
                   jax.ShapeDtypeStruct((m, LANES), F32)],
        compiler_params=_params(("parallel",)),
        name="mamba_in",
    )(x, g, wz, wx, wd)


def _mamba_in_conv_kernel(x_ref, g_ref, wz_ref, wx_ref, wd_ref, cw_ref, cb_ref,
                          gate_ref, act_ref, dt_ref, tail_ref, buf, carry, *, tm, L, nc):
    t = pl.program_id(1)
    xn = _rms(x_ref[...], g_ref[...]).astype(BF16)
    gate_ref[...] = _silu(_dot(xn, wz_ref[...])).astype(BF16)
    dt_ref[...] = _dot(xn, wd_ref[...])
    cw = wx_ref.shape[1] // nc

    @pl.when(t == 0)
    def _():
        carry[...] = jnp.zeros_like(carry)

    for c in range(nc):
        cs = slice(c * cw, (c + 1) * cw)
        buf[0:SUBLANES, :] = carry[:, cs]
        buf[SUBLANES:SUBLANES + tm, :] = _dot(xn, wx_ref[:, cs])
        for r in range(tm // L):
            base = SUBLANES + r * L
            conv = cb_ref[:, cs]
            for k in range(CONV_TAPS):
                lo = base - (CONV_TAPS - 1) + k
                conv = conv + cw_ref[k:k + 1, cs] * buf[lo:lo + L, :]
            act_ref[r * L:(r + 1) * L, cs] = _silu(conv).astype(BF16)
        last = buf[tm:tm + SUBLANES, :]
        carry[:, cs] = last
        tail_ref[0, :, cs] = last


def _mamba_in_conv(x, g, wz, wx, wd, cw, cb, bsz, tm):
    m, d = x.shape
    di, cd = wz.shape[1], wx.shape[1]
    nt = m // bsz // tm
    nc = 4
    row = lambda w: pl.BlockSpec((tm, w), lambda b, t: (b * nt + t, 0))
    return pl.pallas_call(
        functools.partial(_mamba_in_conv_kernel, tm=tm, L=min(SSD_CHUNK, tm), nc=nc),
        grid=(bsz, nt),
        in_specs=[row(d), _const_spec((1, d)), _const_spec((d, di)), _const_spec((d, cd)),
                  _const_spec((d, LANES)), _const_spec((CONV_TAPS, cd)), _const_spec((1, cd))],
        out_specs=[row(di), row(cd), row(LANES), pl.BlockSpec((1, SUBLANES, cd), lambda b, t: (b, 0, 0))],
        out_shape=[jax.ShapeDtypeStruct((m, di), BF16), jax.ShapeDtypeStruct((m, cd), BF16),
                   jax.ShapeDtypeStruct((m, LANES), F32), jax.ShapeDtypeStruct((bsz, SUBLANES, cd), F32)],
        scratch_shapes=[pltpu.VMEM((tm + SUBLANES, cd // nc), F32), pltpu.VMEM((SUBLANES, cd), F32)],
        compiler_params=_params(("parallel", "arbitrary")),
        name="mamba_in_conv",
    )(x, g, wz, wx, wd, cw, cb)


def _ssd_kernel(act_ref, dt_ref, dtb_ref, alog_ref, dsk_ref, tril_ref, e_ref,
                y_ref, h_ref, *, tt, L, G, EH, P, N):
    t = pl.program_id(1)
    di = G * EH * P
    gw = EH * P

    @pl.when(t == 0)
    def _():
        h_ref[...] = jnp.zeros_like(h_ref)

    a = -jnp.exp(alog_ref[...])
    tril = tril_ref[...]
    emat = e_ref[...]
    rows = lax.broadcasted_iota(jnp.int32, (L, L), 0)
    cols = lax.broadcasted_iota(jnp.int32, (L, L), 1)
    causal = rows >= cols
    lane = lax.broadcasted_iota(jnp.int32, (L, gw), 1)
    head_masks = [(lane // P) == e for e in range(EH)]

    for c in range(tt // L):
        rs = slice(c * L, (c + 1) * L)
        dt = _softplus(dt_ref[rs, :] + dtb_ref[...])
        cum = _dot_split_rhs(tril, dt * a, parts=3)
        cum_t = cum.T
        dt_t = dt.T
        to_end = jnp.exp(cum[L - 1:L, :] - cum)
        expanded = _dot_split_lhs(jnp.concatenate([dt * to_end, jnp.exp(cum)], axis=0), emat)
        xw_t = (act_ref[rs, :di].astype(F32) * expanded[:L]).T.astype(BF16)
        chunk_decay = jnp.broadcast_to(jnp.exp(cum_t[:, L - 1:L]), (LANES, N))

        for g in range(G):
            bg = act_ref[rs, di + g * N:di + (g + 1) * N]
            cg = act_ref[rs, di + (G + g) * N:di + (G + g + 1) * N]
            cb = _dot_nt(cg, bg)
            sl = slice(g * gw, (g + 1) * gw)
            xs_g = act_ref[rs, sl]
            mats, blocks = [], []
            for e in range(EH):
                h = g * EH + e
                seg = cum[:, h:h + 1] - cum_t[h:h + 1, :]
                decay = jnp.exp(jnp.where(causal, seg, -jnp.inf))
                mats.append((cb * decay * dt_t[h:h + 1, :]).astype(BF16))
                blocks.append(jnp.where(head_masks[e], xs_g, jnp.zeros_like(xs_g)))
            y_intra = _dot(jnp.concatenate(mats, axis=1), jnp.concatenate(blocks, axis=0))
            hg = h_ref[0, sl, :]
            y_inter = _dot_nt(cg, hg.astype(BF16)) * expanded[L:, sl]
            y_ref[rs, sl] = (y_intra + y_inter + dsk_ref[:, sl] * xs_g.astype(F32)).astype(y_ref.dtype)
            s_new = _dot(xw_t[sl, :], bg)
            for e in range(EH):
                h = g * EH + e
                hs = slice(e * P, (e + 1) * P)
                h_ref[0, g * gw + e * P:g * gw + (e + 1) * P, :] = hg[hs, :] * chunk_decay[h:h + 1, :] + s_new[hs, :]


def _ssd(act, dt_raw, dtb, alog, dsk, bsz, G, EH, P, N, tt):
    m, cd = act.shape
    nt = m // bsz // tt
    L = min(SSD_CHUNK, tt)
    heads = G * EH
    di = heads * P
    tril = jnp.asarray(np.tril(np.ones((L, L), np.float32)), BF16)
    emat, _ = _head_expand_mats(heads, P)
    row = lambda w: pl.BlockSpec((tt, w), lambda b, t: (b * nt + t, 0))
    return pl.pallas_call(
        functools.partial(_ssd_kernel, tt=tt, L=L, G=G, EH=EH, P=P, N=N),
        grid=(bsz, nt),
        in_specs=[row(cd), row(LANES), _const_spec((1, LANES)), _const_spec((1, LANES)), _const_spec((1, di)),
                  _const_spec((L, L)), _const_spec((LANES, di))],
        out_specs=[row(di), pl.BlockSpec((1, di, N), lambda b, t: (b, 0, 0))],
        out_shape=[jax.ShapeDtypeStruct((m, di), BF16), jax.ShapeDtypeStruct((bsz, di, N), F32)],
        compiler_params=_params(("parallel", "arbitrary")),
        name="ssd",
    )(act, dt_raw, dtb, alog, dsk, tril, emat)


def _ssm_step_kernel(*refs, bb, G, EH, P, N, aliased):
    (h0_ref, xbc_ref, c0_ref, dt_ref, cw_ref, cb_ref, dtb_ref, alog_ref, dsk_ref, e_ref, et_ref) = refs[:11]
    rest = refs[12:] if aliased else refs[11:]
    hn_ref, y_ref, cnew_ref, src_hi, src_mid, src_lo, b_scr, c_scr, xs_scr = rest
    i = pl.program_id(0)
    cd = xbc_ref.shape[1]
    di = G * EH * P
    gw = EH * P
    db = xbc_ref.shape[0]

    @pl.when(i == 0)
    def _():
        x = xbc_ref[...]
        conv = cb_ref[...] + cw_ref[CONV_TAPS - 1:CONV_TAPS, :] * x
        for k in range(CONV_TAPS - 1):
            conv = conv + cw_ref[k:k + 1, :] * c0_ref[:, k * cd:(k + 1) * cd]
        cnew_ref[:, 0:(CONV_TAPS - 2) * cd] = c0_ref[:, cd:(CONV_TAPS - 1) * cd]
        cnew_ref[:, (CONV_TAPS - 2) * cd:(CONV_TAPS - 1) * cd] = x
        act = _silu(conv)
        xs = act[:, :di]
        xs_scr[...] = xs
        b_scr[...] = act[:, di:di + G * N]
        c_scr[...] = act[:, di + G * N:]
        dt = _softplus(dt_ref[...] + dtb_ref[...])
        decay = jnp.exp(dt * (-jnp.exp(alog_ref[...])))
        xdt_t = (xs * _dot_split_lhs(dt, e_ref[...], parts=3)).T
        decay_t = _dot_split_rhs(et_ref[...], decay.T, parts=3)
        for part, src in zip(_split3(jnp.concatenate([xdt_t, decay_t], axis=0)), (src_hi, src_mid, src_lo)):
            src[...] = part

    for j in range(bb):
        b = i * bb + j
        onehot = jnp.where(lax.broadcasted_iota(jnp.int32, (db, LANES), 0) == b, 1.0, 0.0).astype(BF16)
        col = _dot(src_hi[...], onehot) + _dot(src_mid[...], onehot) + _dot(src_lo[...], onehot)
        brow = b_scr[pl.ds(b, 1), :]
        crow = c_scr[pl.ds(b, 1), :]
        parts = []
        for g in range(G):
            sl = slice(g * gw, (g + 1) * gw)
            hn = h0_ref[j, sl, :] * col[di + g * gw:di + (g + 1) * gw, :] + col[sl, :] * brow[:, g * N:(g + 1) * N]
            hn_ref[j, sl, :] = hn
            c8 = jnp.broadcast_to(crow[:, g * N:(g + 1) * N], (SUBLANES, N)).astype(BF16)
            parts.append(_dot_nt(c8, hn.astype(BF16))[0:1, :])
        y_ref[pl.ds(b, 1), :] = jnp.concatenate(parts, axis=1) + dsk_ref[...] * xs_scr[pl.ds(b, 1), :]


def _ssm_step(h_all, layer, h_prev_out, xbc, c0, dt_raw, cw, cb, dtb, alog, dsk, G, EH, P, N, bb):
    db, cd = xbc.shape
    hp = G * EH * P
    di = hp
    emat, emat_t = _head_expand_mats(G * EH, P)
    st = pl.BlockSpec((None, bb, hp, N), lambda i: (layer, i, 0, 0))
    taps = CONV_TAPS - 1
    in_specs = [st, _const_spec((db, cd)), _const_spec((db, taps * cd)), _const_spec((db, LANES)),
                _const_spec((CONV_TAPS, cd)), _const_spec((1, cd)), _const_spec((1, LANES)), _const_spec((1, LANES)),
                _const_spec((1, di)), _const_spec((LANES, di)), _const_spec((di, LANES))]
    args = [h_all, xbc, c0, dt_raw, cw, cb, dtb, alog, dsk, emat, emat_t]
    aliases = {}
    if h_prev_out is not None:
        in_specs.append(pl.BlockSpec(memory_space=pl.ANY))
        args.append(h_prev_out)
        aliases = {len(args) - 1: 0}
    return pl.pallas_call(
        functools.partial(_ssm_step_kernel, bb=bb, G=G, EH=EH, P=P, N=N, aliased=h_prev_out is not None),
        grid=(db // bb,),
        in_specs=in_specs,
        out_specs=[st, pl.BlockSpec((db, di), lambda i: (0, 0)), pl.BlockSpec((db, taps * cd), lambda i: (0, 0))],
        out_shape=[jax.ShapeDtypeStruct(h_all.shape, F32), jax.ShapeDtypeStruct((db, di), F32),
                   jax.ShapeDtypeStruct((db, taps * cd), F32)],
        scratch_shapes=[pltpu.VMEM((2 * hp, db), BF16)] * 3
                       + [pltpu.VMEM((db, G * N), F32), pltpu.VMEM((db, G * N), F32), pltpu.VMEM((db, di), F32)],
        input_output_aliases=aliases,
        compiler_params=_params(("arbitrary",)),
        name="ssm_step",
    )(*args)


def _attn_kernel(q_ref, ka_ref, kb_ref, va_ref, vb_ref, sink_ref, o_ref, *, tq, W, KV, GQ, hd):
    t = pl.program_id(1)
    per = LANES // hd
    nq = GQ * W
    key = lax.broadcasted_iota(jnp.int32, (W, nq), 0)
    r = lax.broadcasted_iota(jnp.int32, (W, nq), 1) % W
    cur_ok = key <= r
    half_k = lax.broadcasted_iota(jnp.int32, (2 * W, LANES), 1) // hd
    half_o = lax.broadcasted_iota(jnp.int32, (LANES, nq), 0) // hd
    member = lax.broadcasted_iota(jnp.int32, (1, nq), 1) // W
    nblk = tq // W
    for j in range(nblk):
        first = jnp.logical_and(t == 0, j == 0)
        prev_ok = key >= jnp.maximum(r, jnp.where(first, W, 0))
        rows_prev = slice(j * W, (j + 1) * W)
        if j + 1 < nblk:
            rows_cur = slice((j + 1) * W, (j + 2) * W)
            k2 = jnp.concatenate([ka_ref[0, rows_prev, :], ka_ref[0, rows_cur, :]], axis=0)
            v2 = jnp.concatenate([va_ref[0, rows_prev, :], va_ref[0, rows_cur, :]], axis=0)
        else:
            k2 = jnp.concatenate([ka_ref[0, rows_prev, :], kb_ref[0]], axis=0)
            v2 = jnp.concatenate([va_ref[0, rows_prev, :], vb_ref[0]], axis=0)
        for blk in range(KV // per):
            ls = slice(blk * LANES, (blk + 1) * LANES)
            qs = jnp.concatenate([q_ref[j * W:(j + 1) * W, (e * (KV // per) + blk) * LANES:(e * (KV // per) + blk + 1) * LANES]
                                  for e in range(GQ)], axis=0)
            kblk = k2[:, ls]
            v_t = v2[:, ls].T.astype(BF16)
            acc_t = None
            for u in range(per):
                kh = blk * per + u
                sink = jnp.zeros((1, nq), F32)
                for e in range(GQ):
                    sink = jnp.where(member == e, sink_ref[:, e * KV + kh:e * KV + kh + 1], sink)
                s_t = _dot_nt(jnp.where(half_k == u, kblk, 0.0).astype(BF16), qs)
                s_prev = jnp.where(prev_ok, s_t[:W], -jnp.inf)
                s_cur = jnp.where(cur_ok, s_t[W:], -jnp.inf)
                mx = jnp.maximum(jnp.max(jnp.maximum(s_prev, s_cur), axis=0, keepdims=True), sink)
                p_prev = jnp.exp(s_prev - mx)
                p_cur = jnp.exp(s_cur - mx)
                den = jnp.sum(p_prev + p_cur, axis=0, keepdims=True) + jnp.exp(sink - mx)
                o_t = _dot(v_t, jnp.concatenate([p_prev, p_cur], axis=0).astype(BF16)) / den
                acc_t = o_t if acc_t is None else jnp.where(half_o == u, o_t, acc_t)
            acc = acc_t.T
            for e in range(GQ):
                cb = (e * (KV // per) + blk) * LANES
                o_ref[j * W:(j + 1) * W, cb:cb + LANES] = acc[e * W:(e + 1) * W, :]


def _attn(q, k_all, v_all, sinks, bsz, W, KV, GQ, hd, tq):
    m, dq = q.shape
    nt = m // bsz // tq
    dk = KV * hd
    qrow = pl.BlockSpec((tq, dq), lambda b, t: (b * nt + t, 0))
    main = pl.BlockSpec((1, tq, dk), lambda b, t: (b, t, 0))
    tail = pl.BlockSpec((1, W, dk), lambda b, t: (b, (t + 1) * (tq // W), 0))
    return pl.pallas_call(
        functools.partial(_attn_kernel, tq=tq, W=W, KV=KV, GQ=GQ, hd=hd),
        grid=(bsz, nt),
        in_specs=[qrow, main, tail, main, tail, _const_spec((1, LANES))],
        out_specs=qrow,
        out_shape=jax.ShapeDtypeStruct((m, dq), F32),
        compiler_params=_params(("parallel", "parallel")),
        name="attn",
    )(q, k_all, k_all, v_all, v_all, sinks)


def _attn_step_kernel(q_ref, kc_ref, vc_ref, kn_ref, vn_ref, sink_ref, o_ref, *, bb, KV, GQ, hd):
    i = pl.program_id(0)
    nh = KV * GQ
    dq = nh * hd
    dk = KV * hd
    lane_q = lax.broadcasted_iota(jnp.int32, (nh, dq), 1)
    row_q = lax.broadcasted_iota(jnp.int32, (nh, dq), 0)
    own_q = (lane_q // hd) == row_q
    lane_k = lax.broadcasted_iota(jnp.int32, (nh, dk), 1)
    row_k = lax.broadcasted_iota(jnp.int32, (nh, dk), 0)
    own_k = (lane_k // hd) == (row_k % KV)
    sink = sink_ref[...]
    for j in range(bb):
        b = i * bb + j
        qm = jnp.where(own_q, jnp.broadcast_to(q_ref[pl.ds(b, 1), :], (nh, dq)), 0.0)
        qbd = qm[:, 0:dk]
        for e in range(1, GQ):
            qbd = qbd + qm[:, e * dk:(e + 1) * dk]
        s = _dot_nt(qbd.astype(BF16), kc_ref[j].astype(BF16))
        s_new = jnp.sum(qbd * kn_ref[pl.ds(b, 1), :], axis=-1, keepdims=True)
        mx = jnp.maximum(jnp.maximum(jnp.max(s, axis=-1, keepdims=True), s_new), sink)
        p = jnp.exp(s - mx)
        p_new = jnp.exp(s_new - mx)
        den = jnp.sum(p, axis=-1, keepdims=True) + p_new + jnp.exp(sink - mx)
        o = (_dot(p.astype(BF16), vc_ref[j].astype(BF16)) + p_new * vn_ref[pl.ds(b, 1), :]) / den
        om = jnp.where(own_k, o, 0.0)
        pieces = [jnp.sum(om[e * KV:(e + 1) * KV, :], axis=0, keepdims=True) for e in range(GQ)]
        o_ref[pl.ds(b, 1), :] = jnp.concatenate(pieces, axis=1)


def _attn_step(q, kc, vc, kn, vn, sinks_col, KV, GQ, hd, bb):
    db, dq = q.shape
    W, dk = kc.shape[1], kc.shape[2]
    nh = KV * GQ
    cache = pl.BlockSpec((bb, W, dk), lambda i: (i, 0, 0))
    return pl.pallas_call(
        functools.partial(_attn_step_kernel, bb=bb, KV=KV, GQ=GQ, hd=hd),
        grid=(db // bb,),
        in_specs=[_const_spec((db, dq)), cache, cache, _const_spec((db, dk)), _const_spec((db, dk)),
                  _const_spec((nh, 1))],
        out_specs=pl.BlockSpec((db, dq), lambda i: (0, 0)),
        out_shape=jax.ShapeDtypeStruct((db, dq), F32),
        compiler_params=_params(("arbitrary",)),
        name="attn_step",
    )(q, kc, vc, kn, vn, sinks_col)


def _rope_tables(pos, hd):
    half = hd // 2
    inv = ROPE_THETA ** (-jnp.arange(half, dtype=F32) / half)
    ang = pos.astype(F32)[:, None] * inv[None, :]
    cos, sin = jnp.cos(ang), jnp.sin(ang)
    reps = LANES // hd
    return (jnp.tile(jnp.concatenate([cos, cos], axis=1), (1, reps)),
            jnp.tile(jnp.concatenate([-sin, sin], axis=1), (1, reps)))


def _trunk(x, p, ssm0, conv0, k_hist, v_hist, pos0, w, dims, is_prompt):
    bsz, t_len, d = x.shape
    m = bsz * t_len
    depth, n_a, G, EH, P, N, KV, GQ, hd, W = dims
    hp = G * EH * P
    tm = 512 if t_len % 512 == 0 else m
    x = x.reshape(m, d)
    pos = pos0 + (jnp.arange(t_len) if is_prompt else jnp.zeros((m,), jnp.int32))
    cos, sin = _rope_tables(pos, hd)
    new_conv = []
    ssm_out = None
    k_all = v_all = k_new = v_new = None
    mix_args = None
    for i in range(depth):
        if i < n_a:
            x = _pre(x, w['ffn1_norm'][i], w['ffn1_wi'][i], w['ffn1_wo'][i], tm)[0]
            win = (w['mix_norm'][i], w['ssm_wz'][i], w['ssm_wx'][i], w['ssm_wd'][i])
            sargs = (w['ssm_dt_bias'][i], w['ssm_a_log'][i], w['ssm_dsk'][i])
            cd = w['ssm_wx'][i].shape[1]
            if is_prompt:
                gate, act, dt_raw, tail = _mamba_in_conv(x, *win, w['ssm_conv_w'][i], w['ssm_conv_b'][i], bsz, tm)
                y, h_last = _ssd(act, dt_raw, *sargs, bsz, G, EH, P, N, tt=min(256, t_len))
                ssm_out = h_last[None] if ssm_out is None else jnp.concatenate([ssm_out, h_last[None]], axis=0)
                new_conv.append(tail[:, SUBLANES - (CONV_TAPS - 1):])
            else:
                gate, xbc, dt_raw = _mamba_in(x, *win, tm)
                ssm_out, y, c_last = _ssm_step(ssm0, i, ssm_out, xbc, conv0[i].reshape(bsz, (CONV_TAPS - 1) * cd), dt_raw,
                                               w['ssm_conv_w'][i], w['ssm_conv_b'][i], *sargs, G, EH, P, N, bb=4)
                new_conv.append(c_last.reshape(bsz, CONV_TAPS - 1, cd))
            mix_args = (y, gate, w['ssm_norm'][i], w['ssm_out'][i], G)
        else:
            j = i - n_a
            qargs = (w['mix_norm'][i], w['w_q'][j], w['q_norm'][j], cos, sin, hd)
            x, q = _pre(x, w['ffn1_norm'][i], w['ffn1_wi'][i], w['ffn1_wo'][i], tm, qargs,
                        q_dtype=BF16 if is_prompt else F32)
            if is_prompt:
                o = _attn(q, k_all, v_all, w['sinks_row'][j], bsz, W, KV, GQ, hd, tq=min(256, t_len))
            else:
                o = _attn_step(q, k_hist, v_hist, k_new, v_new, w['sinks_col'][j], KV, GQ, hd, bb=8)
            mix_args = (o, w['w_o'][j])
        kv_args = (w['kv_norm'], w['w_k'], w['w_v'], w['k_norm'], cos, sin, hd) if i == n_a - 1 else None
        res = _post(x, mix_args, (w['ffn2_norm'][i], w['ffn2_wi'][i], w['ffn2_wo'][i]),
                    (w['ple_norm'][i], w['ple_gate'][i], p[i].reshape(m, -1), w['ple_proj'][i]), tm, kv_args)
        x = res[0]
        if kv_args is not None:
            dk = KV * hd
            k_new, v_new = res[1], res[2]
            k_all = jnp.concatenate([k_hist, k_new.reshape(bsz, t_len, dk)], axis=1)
            v_all = jnp.concatenate([v_hist, v_new.reshape(bsz, t_len, dk)], axis=1)
    k_out = k_all[:, -W:].reshape(bsz, W, KV, hd)
    v_out = v_all[:, -W:].reshape(bsz, W, KV, hd)
    return (x.reshape(bsz, t_len, d), ssm_out.reshape(n_a, bsz, G * EH, P, N), jnp.stack(new_conv), k_out, v_out)


def kernel(x_prompt, x_sample, state_ssm, state_conv, cache_k, cache_v, p_prompt, p_sample, ffn1_norm, ffn1_wi, ffn1_wo, mix_norm, ffn2_norm, ffn2_wi, ffn2_wo, ple_norm, ple_gate, ple_proj, ssm_in, ssm_conv_w, ssm_conv_b, ssm_dt_bias, ssm_a_log, ssm_d, ssm_norm, ssm_out, kv_norm, w_kv, k_norm, w_q, q_norm, attn_sinks, w_o):
    depth, d = ffn1_norm.shape
    n_a, heads = ssm_a_log.shape
    di = ssm_norm.shape[1]
    cd = ssm_conv_b.shape[1]
    N = state_ssm.shape[-1]
    P = state_ssm.shape[-2]
    G = (cd - di) // (2 * N)
    EH = heads // G
    W, KV, hd = cache_k.shape[1], cache_k.shape[2], cache_k.shape[3]
    nh = w_q.shape[2] // hd
    GQ = nh // KV
    dims = (depth, n_a, G, EH, P, N, KV, GQ, hd, W)

    slot_head = np.array([(s % KV) * GQ + s // KV for s in range(nh)])
    col_perm = (slot_head[:, None] * hd + np.arange(hd)[None, :]).reshape(-1)
    row1 = lambda v: v[:, None, :]
    sinks_slot = attn_sinks[:, slot_head]
    dk = KV * hd
    w = {
        'ffn1_norm': row1(ffn1_norm), 'ffn1_wi': ffn1_wi.astype(BF16), 'ffn1_wo': ffn1_wo.astype(BF16),
        'ffn2_norm': row1(ffn2_norm), 'ffn2_wi': ffn2_wi.astype(BF16), 'ffn2_wo': ffn2_wo.astype(BF16),
        'mix_norm': row1(mix_norm), 'ple_norm': row1(ple_norm),
        'ple_gate': ple_gate.astype(BF16), 'ple_proj': ple_proj.astype(BF16),
        'ssm_wz': ssm_in[:, :, :di].astype(BF16), 'ssm_wx': ssm_in[:, :, di:di + cd].astype(BF16),
        'ssm_wd': jnp.pad(ssm_in[:, :, di + cd:], ((0, 0), (0, 0), (0, LANES - heads))).astype(BF16),
        'ssm_conv_w': ssm_conv_w, 'ssm_conv_b': row1(ssm_conv_b),
        'ssm_dt_bias': jnp.pad(ssm_dt_bias, ((0, 0), (0, LANES - heads)))[:, None, :],
        'ssm_a_log': jnp.pad(ssm_a_log, ((0, 0), (0, LANES - heads)))[:, None, :],
        'ssm_dsk': jnp.repeat(ssm_d, P, axis=1)[:, None, :],
        'ssm_norm': row1(ssm_norm), 'ssm_out': ssm_out.astype(BF16),
        'kv_norm': kv_norm[None, :], 'w_k': w_kv[:, :dk].astype(BF16), 'w_v': w_kv[:, dk:].astype(BF16),
        'k_norm': jnp.tile(k_norm, KV)[None, :],
        'w_q': w_q[:, :, col_perm].astype(BF16), 'q_norm': jnp.tile(q_norm, (1, nh))[:, None, :],
        'sinks_row': jnp.pad(sinks_slot, ((0, 0), (0, LANES - nh)))[:, None, :],
        'sinks_col': sinks_slot[:, :, None],
        'w_o': w_o[:, col_perm, :].astype(BF16),
    }
    bp = x_prompt.shape[0]
    bs = x_sample.shape[0]
    kv0 = jnp.zeros((bp, W, dk), F32)
    y_p, ssm_p, conv_p, k_p, v_p = _trunk(x_prompt, p_prompt, None, None, kv0, kv0, 0, w, dims, True)
    y_s, ssm_s, conv_s, k_s, v_s = _trunk(x_sample, p_sample, state_ssm.reshape(n_a, bs, heads * P, N), state_conv,
                                          cache_k.reshape(bs, W, dk), cache_v.reshape(bs, W, dk), PAST_LEN, w, dims, False)
    return (y_p, y_s, ssm_p, conv_p, k_p, v_p, ssm_s, conv_s, k_s, v_s)
```

```python
import functools

import numpy as np
import jax
import jax.numpy as jnp
from jax import lax
from jax.experimental import pallas as pl
from jax.experimental.pallas import tpu as pltpu

F32 = jnp.float32
BF16 = jnp.bfloat16
EPS = 1e-6
PAST_LEN = 8192
ROPE_THETA = 10000.0
LANES = 128
SUBLANES = 8
SSD_CHUNK = 128
CONV_TAPS = 4
VMEM_BYTES = 56 * 1024 * 1024


def _dot(a, b):
    return jnp.dot(a, b, preferred_element_type=F32)


def _dot_nt(a, b):
    return lax.dot_general(a, b, (((1,), (1,)), ((), ())), preferred_element_type=F32)


def _split3(a):
    hi = a.astype(BF16)
    r1 = a - hi.astype(F32)
    mid = r1.astype(BF16)
    lo = (r1 - mid.astype(F32)).astype(BF16)
    return hi, mid, lo


def _dot_split_lhs(a, b_exact, parts=2):
    hi, mid, lo = _split3(a)
    out = _dot(hi, b_exact) + _dot(mid, b_exact)
    if parts == 3:
        out = out + _dot(lo, b_exact)
    return out


def _dot_split_rhs(a_exact, b, parts=2):
    hi, mid, lo = _split3(b)
    out = _dot(a_exact, hi) + _dot(a_exact, mid)
    if parts == 3:
        out = out + _dot(a_exact, lo)
    return out


def _rms(x, g):
    ms = jnp.mean(x * x, axis=-1, keepdims=True)
    return x * lax.rsqrt(ms + EPS) * g


def _silu(x):
    return x * jax.nn.sigmoid(x)


def _softplus(v):
    return jnp.maximum(v, 0.0) + jnp.log1p(jnp.exp(-jnp.abs(v)))


def _seg_rms(x, seg_sum, seg_exp, seg):
    ms = _dot((x * x).astype(BF16), seg_sum) * (1.0 / seg)
    return _dot_split_lhs(lax.rsqrt(ms + EPS), seg_exp)


def _rope(x, cos, sin_signed, half):
    width = x.shape[-1]
    lane = lax.broadcasted_iota(jnp.int32, x.shape, 1)
    first = (lane % (2 * half)) < half
    rot = jnp.where(first, pltpu.roll(x, width - half, 1), pltpu.roll(x, half, 1))
    return x * cos + rot * sin_signed


def _const_spec(shape):
    zeros = (0,) * len(shape)
    return pl.BlockSpec(shape, lambda *_: zeros, pipeline_mode=pl.Buffered(1))


def _params(sem):
    return pltpu.CompilerParams(dimension_semantics=sem, vmem_limit_bytes=VMEM_BYTES)


def _seg_mats(width, seg):
    s = np.zeros((width, LANES), np.float32)
    s[np.arange(width), np.arange(width) // seg] = 1.0
    return jnp.asarray(s, BF16), jnp.asarray(s.T.copy(), BF16)


def _head_expand_mats(heads, p):
    e = np.zeros((LANES, heads * p), np.float32)
    e[np.arange(heads * p) // p, np.arange(heads * p)] = 1.0
    return jnp.asarray(e, BF16), jnp.asarray(e.T.copy(), BF16)


def _half_swiglu(x, g_ref, wi_ref, wo_ref, h_scr, dff, fc, rs=slice(None)):
    xn = _rms(x, g_ref[...]).astype(BF16)
    for c in range(dff // fc):
        gt = _dot(xn, wi_ref[:, c * fc:(c + 1) * fc])
        up = _dot(xn, wi_ref[:, dff + c * fc:dff + (c + 1) * fc])
        h_scr[rs, c * fc:(c + 1) * fc] = (_silu(gt) * up).astype(BF16)
    return x + 0.5 * _dot(h_scr[rs, :], wo_ref[...])


def _normed_rope(v, gain_ref, ssum_ref, sexp_ref, cos_ref, sin_ref, hd, rs):
    vn = v * _seg_rms(v, ssum_ref[...], sexp_ref[...], hd) * gain_ref[...]
    reps = v.shape[1] // LANES
    return _rope(vn, jnp.tile(cos_ref[rs, :], (1, reps)), jnp.tile(sin_ref[rs, :], (1, reps)), hd // 2)


def _layer_spec(arr, layer):
    shape = tuple(arr.shape[1:])
    index = (layer,) + (0,) * len(shape)
    return pl.BlockSpec((None,) + shape, lambda *_: index, pipeline_mode=pl.Buffered(1))


def _pre_kernel(*refs, with_q, dff, fc, hd, scale):
    x_ref, g_ref, wi_ref, wo_ref = refs[:4]
    if with_q:
        mg_ref, wq_ref, qg_ref, ssum_ref, sexp_ref, cos_ref, sin_ref, o_ref, q_ref, h_scr = refs[4:]
    else:
        o_ref, h_scr = refs[4:]
    tm = x_ref.shape[0]
    nsplit = 2 if (with_q and tm % 512 == 0) else 1
    for sp in range(nsplit):
        rs = slice(sp * (tm // nsplit), (sp + 1) * (tm // nsplit))
        x1 = _half_swiglu(x_ref[rs, :], g_ref, wi_ref, wo_ref, h_scr, dff, fc, rs)
        o_ref[rs, :] = x1
        if with_q:
            q = _dot(_rms(x1, mg_ref[...]).astype(BF16), wq_ref[...])
            qr = _normed_rope(q, qg_ref, ssum_ref, sexp_ref, cos_ref, sin_ref, hd, rs)
            q_ref[rs, :] = (qr * scale).astype(q_ref.dtype)


def _pre(x, layer, g, wi, wo, tm, qargs=None, q_dtype=BF16):
    m, d = x.shape
    dff = wo.shape[1]
    fc = 256 if dff % 256 == 0 else dff
    row = lambda w: pl.BlockSpec((tm, w), lambda i: (i, 0))
    in_specs = [row(d), _layer_spec(g, layer), _layer_spec(wi, layer), _layer_spec(wo, layer)]
    args = [x, g, wi, wo]
    out_specs, out_shape = [row(d)], [jax.ShapeDtypeStruct((m, d), F32)]
    hd = 0
    if qargs is not None:
        mg, wq, qg, qlayer, cos, sin, hd = qargs
        dq = wq.shape[2]
        ssum, sexp = _seg_mats(dq, hd)
        nt = cos.shape[0] // tm
        tab = pl.BlockSpec((tm, LANES), lambda i: (i % nt, 0))
        in_specs += [_layer_spec(mg, layer), _layer_spec(wq, qlayer), _layer_spec(qg, qlayer),
                     _const_spec((dq, LANES)), _const_spec((LANES, dq)), tab, tab]
        args += [mg, wq, qg, ssum, sexp, cos, sin]
        out_specs.append(row(dq))
        out_shape.append(jax.ShapeDtypeStruct((m, dq), q_dtype))
    return pl.pallas_call(
        functools.partial(_pre_kernel, with_q=qargs is not None, dff=dff, fc=fc, hd=hd, scale=hd ** -0.5 if hd else 1.0),
        grid=(m // tm,),
        in_specs=in_specs, out_specs=out_specs, out_shape=out_shape,
        scratch_shapes=[pltpu.VMEM((tm, dff), BF16)],
        compiler_params=_params(("parallel",)),
        name="pre",
    )(*args)


def _post_kernel(*refs, mamba, with_kv, dff, fc, gw, hd):
    it = iter(refs)
    x_ref = next(it)
    if mamba:
        y_ref, gate_ref, ng_ref, ssum_ref, sexp_ref, wmix_ref = [next(it) for _ in range(6)]
    else:
        att_ref, wmix_ref = next(it), next(it)
    g2_ref, wi_ref, wo_ref, pg_ref, wg_ref, p_ref, wp_ref = [next(it) for _ in range(7)]
    if with_kv:
        kvg_ref, wk_ref, wv_ref, kg_ref, ksum_ref, kexp_ref, cos_ref, sin_ref = [next(it) for _ in range(8)]
    o_ref = next(it)
    if with_kv:
        k_ref, v_ref = next(it), next(it)
    h_scr = next(it)

    tm = x_ref.shape[0]
    nsplit = 2 if tm % 512 == 0 else 1
    for sp in range(nsplit):
        rs = slice(sp * (tm // nsplit), (sp + 1) * (tm // nsplit))
        if mamba:
            gated = y_ref[rs, :].astype(F32) * gate_ref[rs, :].astype(F32)
            r = _seg_rms(gated, ssum_ref[...], sexp_ref[...], gw)
            mix = _dot((gated * r * ng_ref[...]).astype(BF16), wmix_ref[...])
        else:
            mix = _dot(att_ref[rs, :].astype(BF16), wmix_ref[...])
        x = _half_swiglu(x_ref[rs, :] + mix, g2_ref, wi_ref, wo_ref, h_scr, dff, fc, rs)
        gate = jax.nn.sigmoid(_dot(_rms(x, pg_ref[...]).astype(BF16), wg_ref[...]))
        x = x + gate * _dot(p_ref[rs, :].astype(BF16), wp_ref[...])
        o_ref[rs, :] = x
        if with_kv:
            xn = _rms(x, kvg_ref[...]).astype(BF16)
            v_ref[rs, :] = _dot(xn, wv_ref[...])
            k_ref[rs, :] = _normed_rope(_dot(xn, wk_ref[...]), kg_ref, ksum_ref, kexp_ref, cos_ref, sin_ref, hd, rs)


def _post(x, layer, mix_args, ffn_args, ple_args, tm, kv_args=None):
    m, d = x.shape
    g2, wi, wo = ffn_args
    pg, wg, p, wp = ple_args
    dff = wo.shape[1]
    fc = 256 if dff % 256 == 0 else dff
    dp = p.shape[2]
    row = lambda w: pl.BlockSpec((tm, w), lambda i: (i, 0))
    in_specs, args = [row(d)], [x]
    mamba = len(mix_args) == 6
    gw = hd = 0
    if mamba:
        y, gate, ng, wout, mlayer, groups = mix_args
        di = y.shape[1]
        gw = di // groups
        ssum, sexp = _seg_mats(di, gw)
        in_specs += [row(di), row(di), _layer_spec(ng, mlayer), _const_spec((di, LANES)), _const_spec((LANES, di)),
                     _layer_spec(wout, mlayer)]
        args += [y, gate, ng, ssum, sexp, wout]
    else:
        att, wmix, mlayer = mix_args
        in_specs += [row(att.shape[1]), _layer_spec(wmix, mlayer)]
        args += [att, wmix]
    in_specs += [_layer_spec(g2, layer), _layer_spec(wi, layer), _layer_spec(wo, layer),
                 _layer_spec(pg, layer), _layer_spec(wg, layer),
                 pl.BlockSpec((None, tm, dp), lambda i: (layer, i, 0)), _layer_spec(wp, layer)]
    args += [g2, wi, wo, pg, wg, p, wp]
    out_specs, out_shape = [row(d)], [jax.ShapeDtypeStruct((m, d), F32)]
    if kv_args is not None:
        kvg, wk, wv, kg, cos, sin, hd = kv_args
        dk = wk.shape[1]
        ksum, kexp = _seg_mats(dk, hd)
        nt = cos.shape[0] // tm
        tab = pl.BlockSpec((tm, LANES), lambda i: (i % nt, 0))
        in_specs += [_const_spec((1, d)), _const_spec((d, dk)), _const_spec((d, dk)), _const_spec((1, dk)),
                     _const_spec((dk, LANES)), _const_spec((LANES, dk)), tab, tab]
        args += [kvg, wk, wv, kg, ksum, kexp, cos, sin]
        out_specs += [row(dk), row(dk)]
        out_shape += [jax.ShapeDtypeStruct((m, dk), F32)] * 2
    return pl.pallas_call(
        functools.partial(_post_kernel, mamba=mamba, with_kv=kv_args is not None, dff=dff, fc=fc, gw=gw, hd=hd),
        grid=(m // tm,),
        in_specs=in_specs, out_specs=out_specs, out_shape=out_shape,
        scratch_shapes=[pltpu.VMEM((tm, dff), BF16)],
        compiler_params=_params(("parallel",)),
        name="post",
    )(*args)


def _mamba_in_kernel(x_ref, g_ref, w_ref, wd_ref, gate_ref, xbc_ref, dt_ref, *, nc, di):
    xn = _rms(x_ref[...], g_ref[...]).astype(BF16)
    gate_ref[...] = _silu(_dot(xn, w_ref[:, :di])).astype(BF16)
    cw = xbc_ref.shape[1] // nc
    for c in range(nc):
        xbc_ref[:, c * cw:(c + 1) * cw] = _dot(xn, w_ref[:, di + c * cw:di + (c + 1) * cw])
    dt_ref[...] = _dot(xn, wd_ref[...])


def _mamba_in(x, layer, g, w_in, wd, di, cd, tm):
    m, d = x.shape
    row = lambda w: pl.BlockSpec((tm, w), lambda i: (i, 0))
    return pl.pallas_call(
        functools.partial(_mamba_in_kernel, nc=4, di=di),
        grid=(m // tm,),
        in_specs=[row(d), _layer_spec(g, layer), _layer_spec(w_in, layer), _layer_spec(wd, layer)],
        out_specs=[row(di), row(cd), row(LANES)],
        out_shape=[jax.ShapeDtypeStruct((m, di), BF16), jax.ShapeDtypeStruct((m, cd), F32),
                   jax.ShapeDtypeStruct((m, LANES), F32)],
        compiler_params=_params(("parallel",)),
        name="mamba_in",
    )(x, g, w_in, wd)


def _mamba_in_conv_kernel(x_ref, g_ref, w_ref, wd_ref, cw_ref, cb_ref,
                          gate_ref, act_ref, dt_ref, tail_ref, buf, carry, *, tm, L, nc, di):
    t = pl.program_id(1)
    xn = _rms(x_ref[...], g_ref[...]).astype(BF16)
    gate_ref[...] = _silu(_dot(xn, w_ref[:, :di])).astype(BF16)
    dt_ref[...] = _dot(xn, wd_ref[...])
    cw = act_ref.shape[1] // nc

    @pl.when(t == 0)
    def _():
        carry[...] = jnp.zeros_like(carry)

    for c in range(nc):
        cs = slice(c * cw, (c + 1) * cw)
        buf[0:SUBLANES, :] = carry[:, cs]
        buf[SUBLANES:SUBLANES + tm, :] = _dot(xn, w_ref[:, di + c * cw:di + (c + 1) * cw])
        for r in range(tm // L):
            base = SUBLANES + r * L
            conv = cb_ref[:, cs]
            for k in range(CONV_TAPS):
                lo = base - (CONV_TAPS - 1) + k
                conv = conv + cw_ref[k:k + 1, cs] * buf[lo:lo + L, :]
            act_ref[r * L:(r + 1) * L, cs] = _silu(conv).astype(BF16)
        last = buf[tm:tm + SUBLANES, :]
        carry[:, cs] = last
        tail_ref[0, :, cs] = last


def _mamba_in_conv(x, layer, g, w_in, wd, cw, cb, di, cd, bsz, tm):
    m, d = x.shape
    nt = m // bsz // tm
    nc = 4
    row = lambda w: pl.BlockSpec((tm, w), lambda b, t: (b * nt + t, 0))
    return pl.pallas_call(
        functools.partial(_mamba_in_conv_kernel, tm=tm, L=min(SSD_CHUNK, tm), nc=nc, di=di),
        grid=(bsz, nt),
        in_specs=[row(d), _layer_spec(g, layer), _layer_spec(w_in, layer), _layer_spec(wd, layer),
                  _layer_spec(cw, layer), _layer_spec(cb, layer)],
        out_specs=[row(di), row(cd), row(LANES), pl.BlockSpec((1, SUBLANES, cd), lambda b, t: (b, 0, 0))],
        out_shape=[jax.ShapeDtypeStruct((m, di), BF16), jax.ShapeDtypeStruct((m, cd), BF16),
                   jax.ShapeDtypeStruct((m, LANES), F32), jax.ShapeDtypeStruct((bsz, SUBLANES, cd), F32)],
        scratch_shapes=[pltpu.VMEM((tm + SUBLANES, cd // nc), F32), pltpu.VMEM((SUBLANES, cd), F32)],
        compiler_params=_params(("parallel", "arbitrary")),
        name="mamba_in_conv",
    )(x, g, w_in, wd, cw, cb)


def _ssd_kernel(act_ref, dt_ref, dtb_ref, alog_ref, dsk_ref, tril_ref, e_ref,
                y_ref, h_ref, *, tt, L, G, EH, P, N):
    t = pl.program_id(1)
    di = G * EH * P
    gw = EH * P

    @pl.when(t == 0)
    def _():
        h_ref[...] = jnp.zeros_like(h_ref)

    a = -jnp.exp(alog_ref[...])
    tril = tril_ref[...]
    emat = e_ref[...]
    rows = lax.broadcasted_iota(jnp.int32, (L, L), 0)
    cols = lax.broadcasted_iota(jnp.int32, (L, L), 1)
    causal = rows >= cols
    lane = lax.broadcasted_iota(jnp.int32, (L, gw), 1)
    head_masks = [(lane // P) == e for e in range(EH)]

    for c in range(tt // L):
        rs = slice(c * L, (c + 1) * L)
        dt = _softplus(dt_ref[rs, :] + dtb_ref[...])
        cum = _dot_split_rhs(tril, dt * a, parts=3)
        cum_t = cum.T
        dt_t = dt.T
        to_end = jnp.exp(cum[L - 1:L, :] - cum)
        expanded = _dot_split_lhs(jnp.concatenate([dt * to_end, jnp.exp(cum)], axis=0), emat)
        xw_t = (act_ref[rs, :di].astype(F32) * expanded[:L]).T.astype(BF16)
        chunk_decay = jnp.broadcast_to(jnp.exp(cum_t[:, L - 1:L]), (LANES, N))

        for g in range(G):
            bg = act_ref[rs, di + g * N:di + (g + 1) * N]
            cg = act_ref[rs, di + (G + g) * N:di + (G + g + 1) * N]
            cb = _dot_nt(cg, bg)
            sl = slice(g * gw, (g + 1) * gw)
            xs_g = act_ref[rs, sl]
            mats, blocks = [], []
            for e in range(EH):
                h = g * EH + e
                seg = cum[:, h:h + 1] - cum_t[h:h + 1, :]
                decay = jnp.exp(jnp.where(causal, seg, -jnp.inf))
                mats.append((cb * decay * dt_t[h:h + 1, :]).astype(BF16))
                blocks.append(jnp.where(head_masks[e], xs_g, jnp.zeros_like(xs_g)))
            y_intra = _dot(jnp.concatenate(mats, axis=1), jnp.concatenate(blocks, axis=0))
            hg = h_ref[0, sl, :]
            y_inter = _dot_nt(cg, hg.astype(BF16)) * expanded[L:, sl]
            y_ref[rs, sl] = (y_intra + y_inter + dsk_ref[:, sl] * xs_g.astype(F32)).astype(y_ref.dtype)
            s_new = _dot(xw_t[sl, :], bg)
            for e in range(EH):
                h = g * EH + e
                hs = slice(e * P, (e + 1) * P)
                h_ref[0, g * gw + e * P:g * gw + (e + 1) * P, :] = hg[hs, :] * chunk_decay[h:h + 1, :] + s_new[hs, :]


def _ssd(act, dt_raw, layer, dtb, alog, dsk, bsz, G, EH, P, N, tt):
    m, cd = act.shape
    nt = m // bsz // tt
    L = min(SSD_CHUNK, tt)
    heads = G * EH
    di = heads * P
    tril = jnp.asarray(np.tril(np.ones((L, L), np.float32)), BF16)
    emat, _ = _head_expand_mats(heads, P)
    row = lambda w: pl.BlockSpec((tt, w), lambda b, t: (b * nt + t, 0))
    return pl.pallas_call(
        functools.partial(_ssd_kernel, tt=tt, L=L, G=G, EH=EH, P=P, N=N),
        grid=(bsz, nt),
        in_specs=[row(cd), row(LANES), _layer_spec(dtb, layer), _layer_spec(alog, layer), _layer_spec(dsk, layer),
                  _const_spec((L, L)), _const_spec((LANES, di))],
        out_specs=[row(di), pl.BlockSpec((1, di, N), lambda b, t: (b, 0, 0))],
        out_shape=[jax.ShapeDtypeStruct((m, di), BF16), jax.ShapeDtypeStruct((bsz, di, N), F32)],
        compiler_params=_params(("parallel", "arbitrary")),
        name="ssd",
    )(act, dt_raw, dtb, alog, dsk, tril, emat)


def _ssm_step_kernel(*refs, bb, G, EH, P, N, aliased):
    (h0_ref, xbc_ref, c0_ref, dt_ref, cw_ref, cb_ref, dtb_ref, alog_ref, dsk_ref, e_ref) = refs[:10]
    rest = refs[11:] if aliased else refs[10:]
    hn_ref, y_ref, cnew_ref, xdt_t_scr, dec_t_scr, b_scr, c_scr, xs_scr = rest
    i = pl.program_id(0)
    cd = xbc_ref.shape[1]
    di = G * EH * P
    gw = EH * P
    db = xbc_ref.shape[0]

    @pl.when(i == 0)
    def _():
        x = xbc_ref[...]
        conv = cb_ref[...] + cw_ref[CONV_TAPS - 1:CONV_TAPS, :] * x
        for k in range(CONV_TAPS - 1):
            conv = conv + cw_ref[k:k + 1, :] * c0_ref[:, k * cd:(k + 1) * cd]
        cnew_ref[:, 0:(CONV_TAPS - 2) * cd] = c0_ref[:, cd:(CONV_TAPS - 1) * cd]
        cnew_ref[:, (CONV_TAPS - 2) * cd:(CONV_TAPS - 1) * cd] = x
        act = _silu(conv)
        xs = act[:, :di]
        xs_scr[...] = xs
        b_scr[...] = act[:, di:di + G * N]
        c_scr[...] = act[:, di + G * N:]
        dt = _softplus(dt_ref[...] + dtb_ref[...])
        decay = jnp.exp(dt * (-jnp.exp(alog_ref[...])))
        xdt_t_scr[...] = (xs * _dot_split_lhs(dt, e_ref[...], parts=3)).T
        dec_t_scr[...] = decay.T

    seq = lax.broadcasted_iota(jnp.int32, (1, db), 1)
    for j in range(bb):
        b = i * bb + j
        pick = seq == b
        xcol = jnp.sum(jnp.where(pick, xdt_t_scr[...], 0.0), axis=1, keepdims=True)
        dcol = jnp.sum(jnp.where(pick, dec_t_scr[...], 0.0), axis=1, keepdims=True)
        brow = b_scr[pl.ds(b, 1), :]
        crow = c_scr[pl.ds(b, 1), :]
        parts = []
        for g in range(G):
            bg = brow[:, g * N:(g + 1) * N]
            pieces = []
            for e in range(EH):
                h = g * EH + e
                rows = slice(h * P, (h + 1) * P)
                pieces.append(h0_ref[j, rows, :] * dcol[h:h + 1, :] + xcol[rows, :] * bg)
            hn = jnp.concatenate(pieces, axis=0)
            hn_ref[j, g * gw:(g + 1) * gw, :] = hn
            c8 = jnp.broadcast_to(crow[:, g * N:(g + 1) * N], (SUBLANES, N)).astype(BF16)
            parts.append(_dot_nt(c8, hn.astype(BF16))[0:1, :])
        y_ref[pl.ds(b, 1), :] = jnp.concatenate(parts, axis=1) + dsk_ref[...] * xs_scr[pl.ds(b, 1), :]


def _ssm_step(h_all, layer, h_prev_out, xbc, c0, dt_raw, cw, cb, dtb, alog, dsk, G, EH, P, N, bb):
    db, cd = xbc.shape
    hp = G * EH * P
    di = hp
    emat, _ = _head_expand_mats(G * EH, P)
    st = pl.BlockSpec((None, bb, hp, N), lambda i: (layer, i, 0, 0))
    taps = CONV_TAPS - 1
    in_specs = [st, _const_spec((db, cd)), _layer_spec(c0, layer), _const_spec((db, LANES)),
                _layer_spec(cw, layer), _layer_spec(cb, layer), _layer_spec(dtb, layer), _layer_spec(alog, layer),
                _layer_spec(dsk, layer), _const_spec((LANES, di))]
    args = [h_all, xbc, c0, dt_raw, cw, cb, dtb, alog, dsk, emat]
    aliases = {}
    if h_prev_out is not None:
        in_specs.append(pl.BlockSpec(memory_space=pl.ANY))
        args.append(h_prev_out)
        aliases = {len(args) - 1: 0}
    return pl.pallas_call(
        functools.partial(_ssm_step_kernel, bb=bb, G=G, EH=EH, P=P, N=N, aliased=h_prev_out is not None),
        grid=(db // bb,),
        in_specs=in_specs,
        out_specs=[st, pl.BlockSpec((db, di), lambda i: (0, 0)), pl.BlockSpec((db, taps * cd), lambda i: (0, 0))],
        out_shape=[jax.ShapeDtypeStruct(h_all.shape, F32), jax.ShapeDtypeStruct((db, di), F32),
                   jax.ShapeDtypeStruct((db, taps * cd), F32)],
        scratch_shapes=[pltpu.VMEM((hp, db), F32), pltpu.VMEM((LANES, db), F32),
                        pltpu.VMEM((db, G * N), F32), pltpu.VMEM((db, G * N), F32), pltpu.VMEM((db, di), F32)],
        input_output_aliases=aliases,
        compiler_params=_params(("arbitrary",)),
        name="ssm_step",
    )(*args)


def _attn_kernel(q_ref, kh_ref, km_ref, vh_ref, vm_ref, sink_ref, o_ref, *, tq, W, KV, GQ, hd):
    t = pl.program_id(1)
    per = LANES // hd
    nq = GQ * W
    key = lax.broadcasted_iota(jnp.int32, (W, nq), 0)
    r = lax.broadcasted_iota(jnp.int32, (W, nq), 1) % W
    cur_ok = key <= r
    half_k = lax.broadcasted_iota(jnp.int32, (2 * W, LANES), 1) // hd
    half_o = lax.broadcasted_iota(jnp.int32, (LANES, nq), 0) // hd
    member = lax.broadcasted_iota(jnp.int32, (1, nq), 1) // W
    nblk = tq // W
    for j in range(nblk):
        first = jnp.logical_and(t == 0, j == 0)
        prev_ok = key >= jnp.maximum(r, jnp.where(first, W, 0))
        rows_cur = slice(j * W, (j + 1) * W)
        if j == 0:
            k2 = jnp.concatenate([kh_ref[...], km_ref[rows_cur, :]], axis=0)
            v2 = jnp.concatenate([vh_ref[...], vm_ref[rows_cur, :]], axis=0)
        else:
            k2 = km_ref[(j - 1) * W:(j + 1) * W, :]
            v2 = vm_ref[(j - 1) * W:(j + 1) * W, :]
        for blk in range(KV // per):
            ls = slice(blk * LANES, (blk + 1) * LANES)
            qs = jnp.concatenate([q_ref[j * W:(j + 1) * W, (e * (KV // per) + blk) * LANES:(e * (KV // per) + blk + 1) * LANES]
                                  for e in range(GQ)], axis=0)
            kblk = k2[:, ls]
            v_t = v2[:, ls].T.astype(BF16)
            acc_t = None
            for u in range(per):
                kh = blk * per + u
                sink = jnp.zeros((1, nq), F32)
                for e in range(GQ):
                    sink = jnp.where(member == e, sink_ref[:, e * KV + kh:e * KV + kh + 1], sink)
                s_t = _dot_nt(jnp.where(half_k == u, kblk, 0.0).astype(BF16), qs)
                s_prev = jnp.where(prev_ok, s_t[:W], -jnp.inf)
                s_cur = jnp.where(cur_ok, s_t[W:], -jnp.inf)
                mx = jnp.maximum(jnp.max(jnp.maximum(s_prev, s_cur), axis=0, keepdims=True), sink)
                p_prev = jnp.exp(s_prev - mx)
                p_cur = jnp.exp(s_cur - mx)
                den = jnp.sum(p_prev + p_cur, axis=0, keepdims=True) + jnp.exp(sink - mx)
                o_t = _dot(v_t, jnp.concatenate([p_prev, p_cur], axis=0).astype(BF16)) / den
                acc_t = o_t if acc_t is None else jnp.where(half_o == u, o_t, acc_t)
            acc = acc_t.T
            for e in range(GQ):
                cb = (e * (KV // per) + blk) * LANES
                o_ref[j * W:(j + 1) * W, cb:cb + LANES] = acc[e * W:(e + 1) * W, :]


def _attn(q, k, v, sinks, layer, bsz, W, KV, GQ, hd, tq):
    m, dq = q.shape
    nt = m // bsz // tq
    dk = KV * hd
    per_tile = tq // W
    row = lambda w: pl.BlockSpec((tq, w), lambda b, t: (b * nt + t, 0))
    head = pl.BlockSpec((W, dk), lambda b, t: (jnp.maximum((b * nt + t) * per_tile - 1, 0), 0))
    return pl.pallas_call(
        functools.partial(_attn_kernel, tq=tq, W=W, KV=KV, GQ=GQ, hd=hd),
        grid=(bsz, nt),
        in_specs=[row(dq), head, row(dk), head, row(dk), _layer_spec(sinks, layer)],
        out_specs=row(dq),
        out_shape=jax.ShapeDtypeStruct((m, dq), F32),
        compiler_params=_params(("parallel", "parallel")),
        name="attn",
    )(q, k, k, v, v, sinks)


def _attn_step_kernel(q_ref, kc_ref, vc_ref, kn_ref, vn_ref, sink_ref, o_ref, *, bb, KV, GQ, hd):
    i = pl.program_id(0)
    nh = KV * GQ
    dq = nh * hd
    dk = KV * hd
    lane_q = lax.broadcasted_iota(jnp.int32, (nh, dq), 1)
    row_q = lax.broadcasted_iota(jnp.int32, (nh, dq), 0)
    own_q = (lane_q // hd) == row_q
    lane_k = lax.broadcasted_iota(jnp.int32, (nh, dk), 1)
    row_k = lax.broadcasted_iota(jnp.int32, (nh, dk), 0)
    own_k = (lane_k // hd) == (row_k % KV)
    sink = sink_ref[...]
    for j in range(bb):
        b = i * bb + j
        qm = jnp.where(own_q, jnp.broadcast_to(q_ref[pl.ds(b, 1), :], (nh, dq)), 0.0)
        qbd = qm[:, 0:dk]
        for e in range(1, GQ):
            qbd = qbd + qm[:, e * dk:(e + 1) * dk]
        s = _dot_nt(qbd.astype(BF16), kc_ref[j].astype(BF16))
        s_new = jnp.sum(qbd * kn_ref[pl.ds(b, 1), :], axis=-1, keepdims=True)
        mx = jnp.maximum(jnp.maximum(jnp.max(s, axis=-1, keepdims=True), s_new), sink)
        p = jnp.exp(s - mx)
        p_new = jnp.exp(s_new - mx)
        den = jnp.sum(p, axis=-1, keepdims=True) + p_new + jnp.exp(sink - mx)
        o = (_dot(p.astype(BF16), vc_ref[j].astype(BF16)) + p_new * vn_ref[pl.ds(b, 1), :]) / den
        om = jnp.where(own_k, o, 0.0)
        pieces = [jnp.sum(om[e * KV:(e + 1) * KV, :], axis=0, keepdims=True) for e in range(GQ)]
        o_ref[pl.ds(b, 1), :] = jnp.concatenate(pieces, axis=1)


def _attn_step(q, kc, vc, kn, vn, sinks_col, layer, KV, GQ, hd, bb):
    db, dq = q.shape
    W, dk = kc.shape[1], kc.shape[2]
    cache = pl.BlockSpec((bb, W, dk), lambda i: (i, 0, 0))
    return pl.pallas_call(
        functools.partial(_attn_step_kernel, bb=bb, KV=KV, GQ=GQ, hd=hd),
        grid=(db // bb,),
        in_specs=[_const_spec((db, dq)), cache, cache, _const_spec((db, dk)), _const_spec((db, dk)),
                  _layer_spec(sinks_col, layer)],
        out_specs=pl.BlockSpec((db, dq), lambda i: (0, 0)),
        out_shape=jax.ShapeDtypeStruct((db, dq), F32),
        compiler_params=_params(("arbitrary",)),
        name="attn_step",
    )(q, kc, vc, kn, vn, sinks_col)


def _rope_tables(pos, hd):
    half = hd // 2
    inv = ROPE_THETA ** (-jnp.arange(half, dtype=F32) / half)
    ang = pos.astype(F32)[:, None] * inv[None, :]
    cos, sin = jnp.cos(ang), jnp.sin(ang)
    reps = LANES // hd
    return (jnp.tile(jnp.concatenate([cos, cos], axis=1), (1, reps)),
            jnp.tile(jnp.concatenate([-sin, sin], axis=1), (1, reps)))


def _trunk(x, p, ssm0, conv0, k_hist, v_hist, pos0, w, dims, is_prompt):
    bsz, t_len, d = x.shape
    m = bsz * t_len
    depth, n_a, G, EH, P, N, KV, GQ, hd, W = dims
    di = G * EH * P
    cd = di + 2 * G * N
    dk = KV * hd
    tm = 512 if t_len % 512 == 0 else m
    x = x.reshape(m, d)
    p = p.reshape(depth, m, -1)
    pos = pos0 + (jnp.arange(t_len) if is_prompt else jnp.zeros((m,), jnp.int32))
    cos, sin = _rope_tables(pos, hd)
    ffn1 = (w['ffn1_norm'], w['ffn1_wi'], w['ffn1_wo'])
    new_conv = []
    ssm_out = None
    k_new = v_new = None
    mix_args = None
    for i in range(depth):
        if i < n_a:
            x = _pre(x, i, *ffn1, tm)[0]
            win = (w['mix_norm'], w['ssm_in'], w['ssm_wd'])
            sargs = (w['ssm_dt_bias'], w['ssm_a_log'], w['ssm_dsk'])
            if is_prompt:
                gate, act, dt_raw, tail = _mamba_in_conv(x, i, *win, w['ssm_conv_w'], w['ssm_conv_b'], di, cd, bsz, tm)
                y, h_last = _ssd(act, dt_raw, i, *sargs, bsz, G, EH, P, N, tt=min(256, t_len))
                ssm_out = h_last[None] if ssm_out is None else jnp.concatenate([ssm_out, h_last[None]], axis=0)
                new_conv.append(tail[:, SUBLANES - (CONV_TAPS - 1):])
            else:
                gate, xbc, dt_raw = _mamba_in(x, i, *win, di, cd, tm)
                ssm_out, y, c_last = _ssm_step(ssm0, i, ssm_out, xbc, conv0, dt_raw,
                                               w['ssm_conv_w'], w['ssm_conv_b'], *sargs, G, EH, P, N, bb=4)
                new_conv.append(c_last.reshape(bsz, CONV_TAPS - 1, cd))
            mix_args = (y, gate, w['ssm_norm'], w['ssm_out'], i, G)
        else:
            j = i - n_a
            qargs = (w['mix_norm'], w['w_q'], w['q_norm'], j, cos, sin, hd)
            x, q = _pre(x, i, *ffn1, tm, qargs, q_dtype=BF16 if is_prompt else F32)
            if is_prompt:
                o = _attn(q, k_new, v_new, w['sinks_row'], j, bsz, W, KV, GQ, hd, tq=min(256, t_len))
            else:
                o = _attn_step(q, k_hist, v_hist, k_new, v_new, w['sinks_col'], j, KV, GQ, hd, bb=8)
            mix_args = (o, w['w_o'], j)
        kv_args = (w['kv_norm'], w['w_k'], w['w_v'], w['k_norm'], cos, sin, hd) if i == n_a - 1 else None
        res = _post(x, i, mix_args, (w['ffn2_norm'], w['ffn2_wi'], w['ffn2_wo']),
                    (w['ple_norm'], w['ple_gate'], p, w['ple_proj']), tm, kv_args)
        x = res[0]
        if kv_args is not None:
            k_new, v_new = res[1], res[2]
    if is_prompt:
        k_out = k_new.reshape(bsz, t_len, dk)[:, t_len - W:]
        v_out = v_new.reshape(bsz, t_len, dk)[:, t_len - W:]
    else:
        k_out = jnp.concatenate([k_hist, k_new.reshape(bsz, t_len, dk)], axis=1)[:, -W:]
        v_out = jnp.concatenate([v_hist, v_new.reshape(bsz, t_len, dk)], axis=1)[:, -W:]
    return (x.reshape(bsz, t_len, d), ssm_out.reshape(n_a, bsz, G * EH, P, N), jnp.stack(new_conv),
            k_out.reshape(bsz, W, KV, hd), v_out.reshape(bsz, W, KV, hd))


def kernel(x_prompt, x_sample, state_ssm, state_conv, cache_k, cache_v, p_prompt, p_sample, ffn1_norm, ffn1_wi, ffn1_wo, mix_norm, ffn2_norm, ffn2_wi, ffn2_wo, ple_norm, ple_gate, ple_proj, ssm_in, ssm_conv_w, ssm_conv_b, ssm_dt_bias, ssm_a_log, ssm_d, ssm_norm, ssm_out, kv_norm, w_kv, k_norm, w_q, q_norm, attn_sinks, w_o):
    depth, d = ffn1_norm.shape
    n_a, heads = ssm_a_log.shape
    di = ssm_norm.shape[1]
    cd = ssm_conv_b.shape[1]
    N = state_ssm.shape[-1]
    P = state_ssm.shape[-2]
    G = (cd - di) // (2 * N)
    EH = heads // G
    W, KV, hd = cache_k.shape[1], cache_k.shape[2], cache_k.shape[3]
    nh = w_q.shape[2] // hd
    GQ = nh // KV
    dims = (depth, n_a, G, EH, P, N, KV, GQ, hd, W)

    slot_head = np.array([(s % KV) * GQ + s // KV for s in range(nh)])
    col_perm = (slot_head[:, None] * hd + np.arange(hd)[None, :]).reshape(-1)
    row1 = lambda v: v[:, None, :]
    pad_heads = lambda v: jnp.pad(v, ((0, 0),) * (v.ndim - 1) + ((0, LANES - heads),))
    sinks_slot = attn_sinks[:, slot_head]
    dk = KV * hd
    w = {
        'ffn1_norm': row1(ffn1_norm), 'ffn1_wi': ffn1_wi.astype(BF16), 'ffn1_wo': ffn1_wo.astype(BF16),
        'ffn2_norm': row1(ffn2_norm), 'ffn2_wi': ffn2_wi.astype(BF16), 'ffn2_wo': ffn2_wo.astype(BF16),
        'mix_norm': row1(mix_norm), 'ple_norm': row1(ple_norm),
        'ple_gate': ple_gate.astype(BF16), 'ple_proj': ple_proj.astype(BF16),
        'ssm_in': ssm_in.astype(BF16), 'ssm_wd': pad_heads(ssm_in[:, :, di + cd:]).astype(BF16),
        'ssm_conv_w': ssm_conv_w, 'ssm_conv_b': row1(ssm_conv_b),
        'ssm_dt_bias': row1(pad_heads(ssm_dt_bias)), 'ssm_a_log': row1(pad_heads(ssm_a_log)),
        'ssm_dsk': row1(jnp.repeat(ssm_d, P, axis=1)),
        'ssm_norm': row1(ssm_norm), 'ssm_out': ssm_out.astype(BF16),
        'kv_norm': kv_norm[None, :], 'w_k': w_kv[:, :dk].astype(BF16), 'w_v': w_kv[:, dk:].astype(BF16),
        'k_norm': jnp.tile(k_norm, KV)[None, :],
        'w_q': w_q[:, :, col_perm].astype(BF16), 'q_norm': row1(jnp.tile(q_norm, (1, nh))),
        'sinks_row': row1(jnp.pad(sinks_slot, ((0, 0), (0, LANES - nh)))),
        'sinks_col': sinks_slot[:, :, None],
        'w_o': w_o[:, col_perm, :].astype(BF16),
    }
    bs = x_sample.shape[0]
    y_p, ssm_p, conv_p, k_p, v_p = _trunk(x_prompt, p_prompt, None, None, None, None, 0, w, dims, True)
    y_s, ssm_s, conv_s, k_s, v_s = _trunk(x_sample, p_sample, state_ssm.reshape(n_a, bs, heads * P, N),
                                          state_conv.reshape(n_a, bs, (CONV_TAPS - 1) * cd),
                                          cache_k.reshape(bs, W, dk), cache_v.reshape(bs, W, dk), PAST_LEN, w, dims, False)
    return (y_p, y_s, ssm_p, conv_p, k_p, v_p, ssm_s, conv_s, k_s, v_s)
```

```python
import functools

import numpy as np
import jax
import jax.numpy as jnp
from jax import lax
from jax.experimental import pallas as pl
from jax.experimental.pallas import tpu as pltpu

F32 = jnp.float32
BF16 = jnp.bfloat16
EPS = 1e-6
PAST_LEN = 8192
ROPE_THETA = 10000.0
LANES = 128
SUBLANES = 8
SSD_CHUNK = 128
CONV_TAPS = 4
VMEM_BYTES = 56 * 1024 * 1024


def _dot(a, b):
    return jnp.dot(a, b, preferred_element_type=F32)


def _dot_nt(a, b):
    return lax.dot_general(a, b, (((1,), (1,)), ((), ())), preferred_element_type=F32)


def _split(a, parts):
    pieces = []
    for _ in range(parts):
        piece = a.astype(BF16)
        pieces.append(piece)
        a = a - piece.astype(F32)
    return pieces


def _dot_split_lhs(a, b_stacked, parts=2):
    return _dot(jnp.concatenate(_split(a, parts), axis=1), b_stacked)


def _dot_split_rhs(a_tiled, b, parts=2):
    return _dot(a_tiled, jnp.concatenate(_split(b, parts), axis=0))


def _rms(x, g):
    ms = jnp.mean(x * x, axis=-1, keepdims=True)
    return x * lax.rsqrt(ms + EPS) * g


def _sigmoid(x):
    return 0.5 * jnp.tanh(0.5 * x) + 0.5


def _silu(x):
    half = 0.5 * x
    return half * jnp.tanh(half) + half


def _softplus(v):
    return jnp.maximum(v, 0.0) + jnp.log1p(jnp.exp(-jnp.abs(v)))


def _seg_rms(x, seg_sum, seg_exp, seg):
    ms = _dot((x * x).astype(BF16), seg_sum) * (1.0 / seg)
    return _dot_split_lhs(lax.rsqrt(ms + EPS), seg_exp)


def _rope(x, cos, sin_signed, half):
    width = x.shape[-1]
    lane = lax.broadcasted_iota(jnp.int32, x.shape, 1)
    first = (lane % (2 * half)) < half
    rot = jnp.where(first, pltpu.roll(x, width - half, 1), pltpu.roll(x, half, 1))
    return x * cos + rot * sin_signed


def _const_spec(shape):
    zeros = (0,) * len(shape)
    return pl.BlockSpec(shape, lambda *_: zeros, pipeline_mode=pl.Buffered(1))


def _params(sem):
    return pltpu.CompilerParams(dimension_semantics=sem, vmem_limit_bytes=VMEM_BYTES)


def _seg_mats(width, seg):
    s = np.zeros((width, LANES), np.float32)
    s[np.arange(width), np.arange(width) // seg] = 1.0
    return jnp.asarray(s, BF16), jnp.asarray(np.tile(s.T, (2, 1)), BF16)


def _head_expand_mat(heads, p, parts):
    e = np.zeros((LANES, heads * p), np.float32)
    e[np.arange(heads * p) // p, np.arange(heads * p)] = 1.0
    return jnp.asarray(np.tile(e, (parts, 1)), BF16)


def _half_swiglu(x, g_ref, wi_ref, wo_ref, h_scr, dff, fc, rs=slice(None)):
    xn = _rms(x, g_ref[...]).astype(BF16)
    for c in range(dff // fc):
        gt = _dot(xn, wi_ref[:, c * fc:(c + 1) * fc])
        up = _dot(xn, wi_ref[:, dff + c * fc:dff + (c + 1) * fc])
        h_scr[rs, c * fc:(c + 1) * fc] = (_silu(gt) * up).astype(BF16)
    return x + 0.5 * _dot(h_scr[rs, :], wo_ref[...])


def _normed_rope(v, gain_ref, ssum_ref, sexp_ref, cos_ref, sin_ref, hd, rs):
    vn = v * _seg_rms(v, ssum_ref[...], sexp_ref[...], hd) * gain_ref[...]
    reps = v.shape[1] // LANES
    return _rope(vn, jnp.tile(cos_ref[rs, :], (1, reps)), jnp.tile(sin_ref[rs, :], (1, reps)), hd // 2)


def _layer_spec(arr, layer):
    shape = tuple(arr.shape[1:])
    index = (layer,) + (0,) * len(shape)
    return pl.BlockSpec((None,) + shape, lambda *_: index, pipeline_mode=pl.Buffered(1))


def _pre_kernel(*refs, with_q, dff, fc, hd, scale):
    x_ref, g_ref, wi_ref, wo_ref = refs[:4]
    if with_q:
        mg_ref, wq_ref, qg_ref, ssum_ref, sexp_ref, cos_ref, sin_ref, o_ref, q_ref, h_scr = refs[4:]
    else:
        o_ref, h_scr = refs[4:]
    tm = x_ref.shape[0]
    nsplit = 2 if (with_q and tm % 512 == 0) else 1
    for sp in range(nsplit):
        rs = slice(sp * (tm // nsplit), (sp + 1) * (tm // nsplit))
        x1 = _half_swiglu(x_ref[rs, :], g_ref, wi_ref, wo_ref, h_scr, dff, fc, rs)
        o_ref[rs, :] = x1
        if with_q:
            q = _dot(_rms(x1, mg_ref[...]).astype(BF16), wq_ref[...])
            qr = _normed_rope(q, qg_ref, ssum_ref, sexp_ref, cos_ref, sin_ref, hd, rs)
            q_ref[rs, :] = (qr * scale).astype(q_ref.dtype)


def _pre(x, layer, g, wi, wo, tm, qargs=None, q_dtype=BF16):
    m, d = x.shape
    dff = wo.shape[1]
    fc = 256 if dff % 256 == 0 else dff
    row = lambda w: pl.BlockSpec((tm, w), lambda i: (i, 0))
    in_specs = [row(d), _layer_spec(g, layer), _layer_spec(wi, layer), _layer_spec(wo, layer)]
    args = [x, g, wi, wo]
    out_specs, out_shape = [row(d)], [jax.ShapeDtypeStruct((m, d), F32)]
    hd = 0
    if qargs is not None:
        mg, wq, qg, qlayer, cos, sin, hd = qargs
        dq = wq.shape[2]
        ssum, sexp = _seg_mats(dq, hd)
        nt = cos.shape[0] // tm
        tab = pl.BlockSpec((tm, LANES), lambda i: (i % nt, 0))
        in_specs += [_layer_spec(mg, layer), _layer_spec(wq, qlayer), _layer_spec(qg, qlayer),
                     _const_spec(ssum.shape), _const_spec(sexp.shape), tab, tab]
        args += [mg, wq, qg, ssum, sexp, cos, sin]
        out_specs.append(row(dq))
        out_shape.append(jax.ShapeDtypeStruct((m, dq), q_dtype))
    return pl.pallas_call(
        functools.partial(_pre_kernel, with_q=qargs is not None, dff=dff, fc=fc, hd=hd, scale=hd ** -0.5 if hd else 1.0),
        grid=(m // tm,),
        in_specs=in_specs, out_specs=out_specs, out_shape=out_shape,
        scratch_shapes=[pltpu.VMEM((tm, dff), BF16)],
        compiler_params=_params(("parallel",)),
        name="pre",
    )(*args)


def _post_kernel(*refs, mamba, with_kv, dff, fc, gw, hd):
    it = iter(refs)
    x_ref = next(it)
    if mamba:
        y_ref, gate_ref, ng_ref, ssum_ref, sexp_ref, wmix_ref = [next(it) for _ in range(6)]
    else:
        att_ref, wmix_ref = next(it), next(it)
    g2_ref, wi_ref, wo_ref, pg_ref, wg_ref, p_ref, wp_ref = [next(it) for _ in range(7)]
    if with_kv:
        kvg_ref, wk_ref, wv_ref, kg_ref, ksum_ref, kexp_ref, cos_ref, sin_ref = [next(it) for _ in range(8)]
    o_ref = next(it)
    if with_kv:
        k_ref, v_ref = next(it), next(it)
    h_scr = next(it)

    tm = x_ref.shape[0]
    nsplit = 2 if (mamba and tm % 512 == 0) else 1
    for sp in range(nsplit):
        rs = slice(sp * (tm // nsplit), (sp + 1) * (tm // nsplit))
        if mamba:
            gated = y_ref[rs, :].astype(F32) * gate_ref[rs, :].astype(F32)
            r = _seg_rms(gated, ssum_ref[...], sexp_ref[...], gw)
            mix = _dot((gated * r * ng_ref[...]).astype(BF16), wmix_ref[...])
        else:
            mix = _dot(att_ref[rs, :].astype(BF16), wmix_ref[...])
        x = _half_swiglu(x_ref[rs, :] + mix, g2_ref, wi_ref, wo_ref, h_scr, dff, fc, rs)
        gate = _sigmoid(_dot(_rms(x, pg_ref[...]).astype(BF16), wg_ref[...]))
        x = x + gate * _dot(p_ref[rs, :].astype(BF16), wp_ref[...])
        o_ref[rs, :] = x
        if with_kv:
            xn = _rms(x, kvg_ref[...]).astype(BF16)
            v_ref[rs, :] = _dot(xn, wv_ref[...])
            k_ref[rs, :] = _normed_rope(_dot(xn, wk_ref[...]), kg_ref, ksum_ref, kexp_ref, cos_ref, sin_ref, hd, rs)


def _post(x, layer, mix_args, ffn_args, ple_args, tm, kv_args=None):
    m, d = x.shape
    g2, wi, wo = ffn_args
    pg, wg, p, wp = ple_args
    dff = wo.shape[1]
    fc = 256 if dff % 256 == 0 else dff
    dp = p.shape[2]
    row = lambda w: pl.BlockSpec((tm, w), lambda i: (i, 0))
    in_specs, args = [row(d)], [x]
    mamba = len(mix_args) == 6
    gw = hd = 0
    if mamba:
        y, gate, ng, wout, mlayer, groups = mix_args
        di = y.shape[1]
        gw = di // groups
        ssum, sexp = _seg_mats(di, gw)
        in_specs += [row(di), row(di), _layer_spec(ng, mlayer), _const_spec(ssum.shape), _const_spec(sexp.shape),
                     _layer_spec(wout, mlayer)]
        args += [y, gate, ng, ssum, sexp, wout]
    else:
        att, wmix, mlayer = mix_args
        in_specs += [row(att.shape[1]), _layer_spec(wmix, mlayer)]
        args += [att, wmix]
    in_specs += [_layer_spec(g2, layer), _layer_spec(wi, layer), _layer_spec(wo, layer),
                 _layer_spec(pg, layer), _layer_spec(wg, layer),
                 pl.BlockSpec((None, tm, dp), lambda i: (layer, i, 0)), _layer_spec(wp, layer)]
    args += [g2, wi, wo, pg, wg, p, wp]
    out_specs, out_shape = [row(d)], [jax.ShapeDtypeStruct((m, d), F32)]
    if kv_args is not None:
        kvg, wk, wv, kg, cos, sin, hd = kv_args
        dk = wk.shape[1]
        ksum, kexp = _seg_mats(dk, hd)
        nt = cos.shape[0] // tm
        tab = pl.BlockSpec((tm, LANES), lambda i: (i % nt, 0))
        in_specs += [_const_spec((1, d)), _const_spec((d, dk)), _const_spec((d, dk)), _const_spec((1, dk)),
                     _const_spec(ksum.shape), _const_spec(kexp.shape), tab, tab]
        args += [kvg, wk, wv, kg, ksum, kexp, cos, sin]
        out_specs += [row(dk), row(dk)]
        out_shape += [jax.ShapeDtypeStruct((m, dk), F32)] * 2
    return pl.pallas_call(
        functools.partial(_post_kernel, mamba=mamba, with_kv=kv_args is not None, dff=dff, fc=fc, gw=gw, hd=hd),
        grid=(m // tm,),
        in_specs=in_specs, out_specs=out_specs, out_shape=out_shape,
        scratch_shapes=[pltpu.VMEM((tm, dff), BF16)],
        compiler_params=_params(("parallel",)),
        name="post",
    )(*args)


def _mamba_in_kernel(x_ref, g_ref, w_ref, wd_ref, gate_ref, xbc_ref, dt_ref, *, nc, di):
    xn = _rms(x_ref[...], g_ref[...]).astype(BF16)
    gate_ref[...] = _silu(_dot(xn, w_ref[:, :di])).astype(BF16)
    cw = xbc_ref.shape[1] // nc
    for c in range(nc):
        xbc_ref[:, c * cw:(c + 1) * cw] = _dot(xn, w_ref[:, di + c * cw:di + (c + 1) * cw])
    dt_ref[...] = _dot(xn, wd_ref[...])


def _mamba_in(x, layer, g, w_in, wd, di, cd, tm):
    m, d = x.shape
    row = lambda w: pl.BlockSpec((tm, w), lambda i: (i, 0))
    return pl.pallas_call(
        functools.partial(_mamba_in_kernel, nc=4, di=di),
        grid=(m // tm,),
        in_specs=[row(d), _layer_spec(g, layer), _layer_spec(w_in, layer), _layer_spec(wd, layer)],
        out_specs=[row(di), row(cd), row(LANES)],
        out_shape=[jax.ShapeDtypeStruct((m, di), BF16), jax.ShapeDtypeStruct((m, cd), F32),
                   jax.ShapeDtypeStruct((m, LANES), F32)],
        compiler_params=_params(("parallel",)),
        name="mamba_in",
    )(x, g, w_in, wd)


def _mamba_in_conv_kernel(x_ref, g_ref, w_ref, wd_ref, cw_ref, cb_ref,
                          gate_ref, act_ref, dt_ref, tail_ref, buf, carry, *, tm, L, nc, di):
    t = pl.program_id(1)
    xn = _rms(x_ref[...], g_ref[...]).astype(BF16)
    gate_ref[...] = _silu(_dot(xn, w_ref[:, :di])).astype(BF16)
    dt_ref[...] = _dot(xn, wd_ref[...])
    cw = act_ref.shape[1] // nc

    @pl.when(t == 0)
    def _():
        carry[...] = jnp.zeros_like(carry)

    for c in range(nc):
        cs = slice(c * cw, (c + 1) * cw)
        buf[0:SUBLANES, :] = carry[:, cs]
        buf[SUBLANES:SUBLANES + tm, :] = _dot(xn, w_ref[:, di + c * cw:di + (c + 1) * cw])
        for r in range(tm // L):
            base = SUBLANES + r * L
            conv = cb_ref[:, cs]
            for k in range(CONV_TAPS):
                lo = base - (CONV_TAPS - 1) + k
                conv = conv + cw_ref[k:k + 1, cs] * buf[lo:lo + L, :]
            act_ref[r * L:(r + 1) * L, cs] = _silu(conv).astype(BF16)
        last = buf[tm:tm + SUBLANES, :]
        carry[:, cs] = last
        tail_ref[0, :, cs] = last


def _mamba_in_conv(x, layer, g, w_in, wd, cw, cb, di, cd, bsz, tm):
    m, d = x.shape
    nt = m // bsz // tm
    nc = 4
    row = lambda w: pl.BlockSpec((tm, w), lambda b, t: (b * nt + t, 0))
    return pl.pallas_call(
        functools.partial(_mamba_in_conv_kernel, tm=tm, L=min(SSD_CHUNK, tm), nc=nc, di=di),
        grid=(bsz, nt),
        in_specs=[row(d), _layer_spec(g, layer), _layer_spec(w_in, layer), _layer_spec(wd, layer),
                  _layer_spec(cw, layer), _layer_spec(cb, layer)],
        out_specs=[row(di), row(cd), row(LANES), pl.BlockSpec((1, SUBLANES, cd), lambda b, t: (b, 0, 0))],
        out_shape=[jax.ShapeDtypeStruct((m, di), BF16), jax.ShapeDtypeStruct((m, cd), BF16),
                   jax.ShapeDtypeStruct((m, LANES), F32), jax.ShapeDtypeStruct((bsz, SUBLANES, cd), F32)],
        scratch_shapes=[pltpu.VMEM((tm + SUBLANES, cd // nc), F32), pltpu.VMEM((SUBLANES, cd), F32)],
        compiler_params=_params(("parallel", "arbitrary")),
        name="mamba_in_conv",
    )(x, g, w_in, wd, cw, cb)


def _ssd_kernel(act_ref, dt_ref, dtb_ref, alog_ref, dsk_ref, tril_ref, e_ref,
                y_ref, h_ref, *, tt, L, G, EH, P, N):
    t = pl.program_id(1)
    di = G * EH * P
    gw = EH * P

    @pl.when(t == 0)
    def _():
        h_ref[...] = jnp.zeros_like(h_ref)

    a = -jnp.exp(alog_ref[...])
    tril = tril_ref[...]
    emat = e_ref[...]
    rows = lax.broadcasted_iota(jnp.int32, (L, L), 0)
    cols = lax.broadcasted_iota(jnp.int32, (L, L), 1)
    causal = rows >= cols
    lane = lax.broadcasted_iota(jnp.int32, (L, gw), 1)
    head_masks = [(lane // P) == e for e in range(EH)]

    for c in range(tt // L):
        rs = slice(c * L, (c + 1) * L)
        dt = _softplus(dt_ref[rs, :] + dtb_ref[...])
        cum = _dot_split_rhs(tril, dt * a, parts=3)
        cum_t = cum.T
        dt_t = dt.T
        to_end = jnp.exp(cum[L - 1:L, :] - cum)
        expanded = _dot_split_lhs(jnp.concatenate([dt * to_end, jnp.exp(cum)], axis=0), emat)
        xw_t = (act_ref[rs, :di].astype(F32) * expanded[:L]).T.astype(BF16)
        chunk_decay = jnp.broadcast_to(jnp.exp(cum_t[:, L - 1:L]), (LANES, N))

        for g in range(G):
            bg = act_ref[rs, di + g * N:di + (g + 1) * N]
            cg = act_ref[rs, di + (G + g) * N:di + (G + g + 1) * N]
            cb = _dot_nt(cg, bg)
            sl = slice(g * gw, (g + 1) * gw)
            xs_g = act_ref[rs, sl]
            mats, blocks = [], []
            for e in range(EH):
                h = g * EH + e
                seg = cum[:, h:h + 1] - cum_t[h:h + 1, :]
                decay = jnp.exp(jnp.where(causal, seg, -jnp.inf))
                mats.append((cb * decay * dt_t[h:h + 1, :]).astype(BF16))
                blocks.append(jnp.where(head_masks[e], xs_g, jnp.zeros_like(xs_g)))
            y_intra = _dot(jnp.concatenate(mats, axis=1), jnp.concatenate(blocks, axis=0))
            hg = h_ref[0, sl, :]
            y_inter = _dot_nt(cg, hg.astype(BF16)) * expanded[L:, sl]
            y_ref[rs, sl] = (y_intra + y_inter + dsk_ref[:, sl] * xs_g.astype(F32)).astype(y_ref.dtype)
            s_new = _dot(xw_t[sl, :], bg)
            for e in range(EH):
                h = g * EH + e
                hs = slice(e * P, (e + 1) * P)
                h_ref[0, g * gw + e * P:g * gw + (e + 1) * P, :] = hg[hs, :] * chunk_decay[h:h + 1, :] + s_new[hs, :]


def _ssd(act, dt_raw, layer, dtb, alog, dsk, bsz, G, EH, P, N, tt):
    m, cd = act.shape
    nt = m // bsz // tt
    L = min(SSD_CHUNK, tt)
    heads = G * EH
    di = heads * P
    tril = jnp.asarray(np.tile(np.tril(np.ones((L, L), np.float32)), (1, 3)), BF16)
    emat = _head_expand_mat(heads, P, 2)
    row = lambda w: pl.BlockSpec((tt, w), lambda b, t: (b * nt + t, 0))
    return pl.pallas_call(
        functools.partial(_ssd_kernel, tt=tt, L=L, G=G, EH=EH, P=P, N=N),
        grid=(bsz, nt),
        in_specs=[row(cd), row(LANES), _layer_spec(dtb, layer), _layer_spec(alog, layer), _layer_spec(dsk, layer),
                  _const_spec(tril.shape), _const_spec(emat.shape)],
        out_specs=[row(di), pl.BlockSpec((1, di, N), lambda b, t: (b, 0, 0))],
        out_shape=[jax.ShapeDtypeStruct((m, di), BF16), jax.ShapeDtypeStruct((bsz, di, N), F32)],
        compiler_params=_params(("parallel", "arbitrary")),
        name="ssd",
    )(act, dt_raw, dtb, alog, dsk, tril, emat)


def _ssm_step_kernel(*refs, bb, G, EH, P, N, aliased):
    (h0_ref, xbc_ref, c0_ref, dt_ref, cw_ref, cb_ref, dtb_ref, alog_ref, dsk_ref, e_ref) = refs[:10]
    rest = refs[11:] if aliased else refs[10:]
    hn_ref, y_ref, cnew_ref, xdt_t_scr, dec_t_scr, b_scr, c_scr, xs_scr = rest
    i = pl.program_id(0)
    cd = xbc_ref.shape[1]
    di = G * EH * P
    gw = EH * P
    db = xbc_ref.shape[0]

    @pl.when(i == 0)
    def _():
        x = xbc_ref[...]
        conv = cb_ref[...] + cw_ref[CONV_TAPS - 1:CONV_TAPS, :] * x
        for k in range(CONV_TAPS - 1):
            conv = conv + cw_ref[k:k + 1, :] * c0_ref[:, k * cd:(k + 1) * cd]
        cnew_ref[:, 0:(CONV_TAPS - 2) * cd] = c0_ref[:, cd:(CONV_TAPS - 1) * cd]
        cnew_ref[:, (CONV_TAPS - 2) * cd:(CONV_TAPS - 1) * cd] = x
        act = _silu(conv)
        xs = act[:, :di]
        xs_scr[...] = xs
        b_scr[...] = act[:, di:di + G * N]
        c_scr[...] = act[:, di + G * N:]
        dt = _softplus(dt_ref[...] + dtb_ref[...])
        decay = jnp.exp(dt * (-jnp.exp(alog_ref[...])))
        xdt_t_scr[...] = (xs * _dot_split_lhs(dt, e_ref[...], parts=3)).T
        dec_t_scr[...] = decay.T

    seq = lax.broadcasted_iota(jnp.int32, (1, db), 1)
    for j in range(bb):
        b = i * bb + j
        pick = seq == b
        xcol = jnp.sum(jnp.where(pick, xdt_t_scr[...], 0.0), axis=1, keepdims=True)
        dcol = jnp.sum(jnp.where(pick, dec_t_scr[...], 0.0), axis=1, keepdims=True)
        brow = b_scr[pl.ds(b, 1), :]
        crow = c_scr[pl.ds(b, 1), :]
        parts = []
        for g in range(G):
            bg = brow[:, g * N:(g + 1) * N]
            pieces = []
            for e in range(EH):
                h = g * EH + e
                rows = slice(h * P, (h + 1) * P)
                pieces.append(h0_ref[j, rows, :] * dcol[h:h + 1, :] + xcol[rows, :] * bg)
            hn = jnp.concatenate(pieces, axis=0)
            hn_ref[j, g * gw:(g + 1) * gw, :] = hn
            c8 = jnp.broadcast_to(crow[:, g * N:(g + 1) * N], (SUBLANES, N)).astype(BF16)
            parts.append(_dot_nt(c8, hn.astype(BF16))[0:1, :])
        y_ref[pl.ds(b, 1), :] = jnp.concatenate(parts, axis=1) + dsk_ref[...] * xs_scr[pl.ds(b, 1), :]


def _ssm_step(h_all, layer, h_prev_out, xbc, c0, dt_raw, cw, cb, dtb, alog, dsk, G, EH, P, N, bb):
    db, cd = xbc.shape
    hp = G * EH * P
    di = hp
    emat = _head_expand_mat(G * EH, P, 3)
    st = pl.BlockSpec((None, bb, hp, N), lambda i: (layer, i, 0, 0))
    taps = CONV_TAPS - 1
    in_specs = [st, _const_spec((db, cd)), _layer_spec(c0, layer), _const_spec((db, LANES)),
                _layer_spec(cw, layer), _layer_spec(cb, layer), _layer_spec(dtb, layer), _layer_spec(alog, layer),
                _layer_spec(dsk, layer), _const_spec(emat.shape)]
    args = [h_all, xbc, c0, dt_raw, cw, cb, dtb, alog, dsk, emat]
    aliases = {}
    if h_prev_out is not None:
        in_specs.append(pl.BlockSpec(memory_space=pl.ANY))
        args.append(h_prev_out)
        aliases = {len(args) - 1: 0}
    return pl.pallas_call(
        functools.partial(_ssm_step_kernel, bb=bb, G=G, EH=EH, P=P, N=N, aliased=h_prev_out is not None),
        grid=(db // bb,),
        in_specs=in_specs,
        out_specs=[st, pl.BlockSpec((db, di), lambda i: (0, 0)), pl.BlockSpec((db, taps * cd), lambda i: (0, 0))],
        out_shape=[jax.ShapeDtypeStruct(h_all.shape, F32), jax.ShapeDtypeStruct((db, di), F32),
                   jax.ShapeDtypeStruct((db, taps * cd), F32)],
        scratch_shapes=[pltpu.VMEM((hp, db), F32), pltpu.VMEM((LANES, db), F32),
                        pltpu.VMEM((db, G * N), F32), pltpu.VMEM((db, G * N), F32), pltpu.VMEM((db, di), F32)],
        input_output_aliases=aliases,
        compiler_params=_params(("arbitrary",)),
        name="ssm_step",
    )(*args)


def _attn_kernel(q_ref, kh_ref, km_ref, vh_ref, vm_ref, sink_ref, o_ref, *, tq, W, KV, GQ, hd):
    t = pl.program_id(1)
    per = LANES // hd
    key = lax.broadcasted_iota(jnp.int32, (W, W), 0)
    r = lax.broadcasted_iota(jnp.int32, (W, W), 1)
    cur_ok = key <= r
    half_k = lax.broadcasted_iota(jnp.int32, (2 * W, LANES), 1) // hd
    half_o = lax.broadcasted_iota(jnp.int32, (LANES, W), 0) // hd
    nblk = tq // W
    for j in range(nblk):
        first = jnp.logical_and(t == 0, j == 0)
        prev_ok = key >= jnp.maximum(r, jnp.where(first, W, 0))
        rows_cur = slice(j * W, (j + 1) * W)
        if j == 0:
            k2 = jnp.concatenate([kh_ref[...], km_ref[rows_cur, :]], axis=0)
            v2 = jnp.concatenate([vh_ref[...], vm_ref[rows_cur, :]], axis=0)
        else:
            k2 = km_ref[(j - 1) * W:(j + 1) * W, :]
            v2 = vm_ref[(j - 1) * W:(j + 1) * W, :]
        for blk in range(KV // per):
            ls = slice(blk * LANES, (blk + 1) * LANES)
            kblk = k2[:, ls]
            k_u = [jnp.where(half_k == u, kblk, 0.0).astype(BF16) for u in range(per)]
            v_t = v2[:, ls].T.astype(BF16)
            for e in range(GQ):
                cb = (e * (KV // per) + blk) * LANES
                qs = q_ref[j * W:(j + 1) * W, cb:cb + LANES]
                acc_t = None
                for u in range(per):
                    slot = e * KV + blk * per + u
                    sink = sink_ref[:, slot:slot + 1]
                    s_t = _dot_nt(k_u[u], qs)
                    s_prev = jnp.where(prev_ok, s_t[:W], -jnp.inf)
                    s_cur = jnp.where(cur_ok, s_t[W:], -jnp.inf)
                    mx = jnp.maximum(jnp.max(jnp.maximum(s_prev, s_cur), axis=0, keepdims=True), sink)
                    p_prev = jnp.exp(s_prev - mx)
                    p_cur = jnp.exp(s_cur - mx)
                    den = jnp.sum(p_prev + p_cur, axis=0, keepdims=True) + jnp.exp(sink - mx)
                    o_t = _dot(v_t, jnp.concatenate([p_prev, p_cur], axis=0).astype(BF16)) / den
                    acc_t = o_t if acc_t is None else jnp.where(half_o == u, o_t, acc_t)
                o_ref[j * W:(j + 1) * W, cb:cb + LANES] = acc_t.T


def _attn(q, k, v, sinks, layer, bsz, W, KV, GQ, hd, tq):
    m, dq = q.shape
    nt = m // bsz // tq
    dk = KV * hd
    per_tile = tq // W
    row = lambda w: pl.BlockSpec((tq, w), lambda b, t: (b * nt + t, 0))
    head = pl.BlockSpec((W, dk), lambda b, t: (jnp.maximum((b * nt + t) * per_tile - 1, 0), 0))
    return pl.pallas_call(
        functools.partial(_attn_kernel, tq=tq, W=W, KV=KV, GQ=GQ, hd=hd),
        grid=(bsz, nt),
        in_specs=[row(dq), head, row(dk), head, row(dk), _layer_spec(sinks, layer)],
        out_specs=row(dq),
        out_shape=jax.ShapeDtypeStruct((m, dq), F32),
        compiler_params=_params(("parallel", "parallel")),
        name="attn",
    )(q, k, k, v, v, sinks)


def _attn_step_kernel(q_ref, kc_ref, vc_ref, kn_ref, vn_ref, sink_ref, o_ref, *, bb, KV, GQ, hd):
    i = pl.program_id(0)
    nh = KV * GQ
    dq = nh * hd
    dk = KV * hd
    lane_q = lax.broadcasted_iota(jnp.int32, (nh, dq), 1)
    row_q = lax.broadcasted_iota(jnp.int32, (nh, dq), 0)
    own_q = (lane_q // hd) == row_q
    lane_k = lax.broadcasted_iota(jnp.int32, (nh, dk), 1)
    row_k = lax.broadcasted_iota(jnp.int32, (nh, dk), 0)
    own_k = (lane_k // hd) == (row_k % KV)
    sink = sink_ref[...]
    for j in range(bb):
        b = i * bb + j
        qm = jnp.where(own_q, jnp.broadcast_to(q_ref[pl.ds(b, 1), :], (nh, dq)), 0.0)
        qbd = qm[:, 0:dk]
        for e in range(1, GQ):
            qbd = qbd + qm[:, e * dk:(e + 1) * dk]
        s = _dot_nt(qbd.astype(BF16), kc_ref[j].astype(BF16))
        s_new = jnp.sum(qbd * kn_ref[pl.ds(b, 1), :], axis=-1, keepdims=True)
        mx = jnp.maximum(jnp.maximum(jnp.max(s, axis=-1, keepdims=True), s_new), sink)
        p = jnp.exp(s - mx)
        p_new = jnp.exp(s_new - mx)
        den = jnp.sum(p, axis=-1, keepdims=True) + p_new + jnp.exp(sink - mx)
        o = (_dot(p.astype(BF16), vc_ref[j].astype(BF16)) + p_new * vn_ref[pl.ds(b, 1), :]) / den
        om = jnp.where(own_k, o, 0.0)
        pieces = [jnp.sum(om[e * KV:(e + 1) * KV, :], axis=0, keepdims=True) for e in range(GQ)]
        o_ref[pl.ds(b, 1), :] = jnp.concatenate(pieces, axis=1)


def _attn_step(q, kc, vc, kn, vn, sinks_col, layer, KV, GQ, hd, bb):
    db, dq = q.shape
    W, dk = kc.shape[1], kc.shape[2]
    cache = pl.BlockSpec((bb, W, dk), lambda i: (i, 0, 0))
    return pl.pallas_call(
        functools.partial(_attn_step_kernel, bb=bb, KV=KV, GQ=GQ, hd=hd),
        grid=(db // bb,),
        in_specs=[_const_spec((db, dq)), cache, cache, _const_spec((db, dk)), _const_spec((db, dk)),
                  _layer_spec(sinks_col, layer)],
        out_specs=pl.BlockSpec((db, dq), lambda i: (0, 0)),
        out_shape=jax.ShapeDtypeStruct((db, dq), F32),
        compiler_params=_params(("arbitrary",)),
        name="attn_step",
    )(q, kc, vc, kn, vn, sinks_col)


def _rope_tables(pos, hd):
    half = hd // 2
    inv = ROPE_THETA ** (-jnp.arange(half, dtype=F32) / half)
    ang = pos.astype(F32)[:, None] * inv[None, :]
    cos, sin = jnp.cos(ang), jnp.sin(ang)
    reps = LANES // hd
    return (jnp.tile(jnp.concatenate([cos, cos], axis=1), (1, reps)),
            jnp.tile(jnp.concatenate([-sin, sin], axis=1), (1, reps)))


def _trunk(x, p, ssm0, conv0, k_hist, v_hist, pos0, w, dims, is_prompt):
    bsz, t_len, d = x.shape
    m = bsz * t_len
    depth, n_a, G, EH, P, N, KV, GQ, hd, W = dims
    di = G * EH * P
    cd = di + 2 * G * N
    dk = KV * hd
    tm = 512 if t_len % 512 == 0 else m
    x = x.reshape(m, d)
    p = p.reshape(depth, m, -1)
    pos = pos0 + (jnp.arange(t_len) if is_prompt else jnp.zeros((m,), jnp.int32))
    cos, sin = _rope_tables(pos, hd)
    ffn1 = (w['ffn1_norm'], w['ffn1_wi'], w['ffn1_wo'])
    new_conv = []
    ssm_out = None
    k_new = v_new = None
    mix_args = None
    for i in range(depth):
        if i < n_a:
            win = (w['mix_norm'], w['ssm_in'], w['ssm_wd'])
            sargs = (w['ssm_dt_bias'], w['ssm_a_log'], w['ssm_dsk'])
            x = _pre(x, i, *ffn1, tm)[0]
            if is_prompt:
                gate, act, dt_raw, tail = _mamba_in_conv(x, i, *win, w['ssm_conv_w'], w['ssm_conv_b'], di, cd, bsz, tm)
                y, h_last = _ssd(act, dt_raw, i, *sargs, bsz, G, EH, P, N, tt=min(256, t_len))
                ssm_out = h_last[None] if ssm_out is None else jnp.concatenate([ssm_out, h_last[None]], axis=0)
                new_conv.append(tail[:, SUBLANES - (CONV_TAPS - 1):])
            else:
                gate, xbc, dt_raw = _mamba_in(x, i, *win, di, cd, tm)
                ssm_out, y, c_last = _ssm_step(ssm0, i, ssm_out, xbc, conv0, dt_raw,
                                               w['ssm_conv_w'], w['ssm_conv_b'], *sargs, G, EH, P, N, bb=4)
                new_conv.append(c_last.reshape(bsz, CONV_TAPS - 1, cd))
            mix_args = (y, gate, w['ssm_norm'], w['ssm_out'], i, G)
        else:
            j = i - n_a
            qargs = (w['mix_norm'], w['w_q'], w['q_norm'], j, cos, sin, hd)
            x, q = _pre(x, i, *ffn1, tm, qargs, q_dtype=BF16 if is_prompt else F32)
            if is_prompt:
                o = _attn(q, k_new, v_new, w['sinks_row'], j, bsz, W, KV, GQ, hd, tq=min(256, t_len))
            else:
                o = _attn_step(q, k_hist, v_hist, k_new, v_new, w['sinks_col'], j, KV, GQ, hd, bb=8)
            mix_args = (o, w['w_o'], j)
        kv_args = (w['kv_norm'], w['w_k'], w['w_v'], w['k_norm'], cos, sin, hd) if i == n_a - 1 else None
        res = _post(x, i, mix_args, (w['ffn2_norm'], w['ffn2_wi'], w['ffn2_wo']),
                    (w['ple_norm'], w['ple_gate'], p, w['ple_proj']), tm, kv_args)
        x = res[0]
        if kv_args is not None:
            k_new, v_new = res[1], res[2]
    if is_prompt:
        k_out = k_new.reshape(bsz, t_len, dk)[:, t_len - W:]
        v_out = v_new.reshape(bsz, t_len, dk)[:, t_len - W:]
    else:
        k_out = jnp.concatenate([k_hist, k_new.reshape(bsz, t_len, dk)], axis=1)[:, -W:]
        v_out = jnp.concatenate([v_hist, v_new.reshape(bsz, t_len, dk)], axis=1)[:, -W:]
    return (x.reshape(bsz, t_len, d), ssm_out.reshape(n_a, bsz, G * EH, P, N), jnp.stack(new_conv),
            k_out.reshape(bsz, W, KV, hd), v_out.reshape(bsz, W, KV, hd))


def kernel(x_prompt, x_sample, state_ssm, state_conv, cache_k, cache_v, p_prompt, p_sample, ffn1_norm, ffn1_wi, ffn1_wo, mix_norm, ffn2_norm, ffn2_wi, ffn2_wo, ple_norm, ple_gate, ple_proj, ssm_in, ssm_conv_w, ssm_conv_b, ssm_dt_bias, ssm_a_log, ssm_d, ssm_norm, ssm_out, kv_norm, w_kv, k_norm, w_q, q_norm, attn_sinks, w_o):
    depth, d = ffn1_norm.shape
    n_a, heads = ssm_a_log.shape
    di = ssm_norm.shape[1]
    cd = ssm_conv_b.shape[1]
    N = state_ssm.shape[-1]
    P = state_ssm.shape[-2]
    G = (cd - di) // (2 * N)
    EH = heads // G
    W, KV, hd = cache_k.shape[1], cache_k.shape[2], cache_k.shape[3]
    nh = w_q.shape[2] // hd
    GQ = nh // KV
    dims = (depth, n_a, G, EH, P, N, KV, GQ, hd, W)

    slot_head = np.array([(s % KV) * GQ + s // KV for s in range(nh)])
    col_perm = (slot_head[:, None] * hd + np.arange(hd)[None, :]).reshape(-1)
    row1 = lambda v: v[:, None, :]
    pad_heads = lambda v: jnp.pad(v, ((0, 0),) * (v.ndim - 1) + ((0, LANES - heads),))
    sinks_slot = attn_sinks[:, slot_head]
    dk = KV * hd
    w = {
        'ffn1_norm': row1(ffn1_norm), 'ffn1_wi': ffn1_wi.astype(BF16), 'ffn1_wo': ffn1_wo.astype(BF16),
        'ffn2_norm': row1(ffn2_norm), 'ffn2_wi': ffn2_wi.astype(BF16), 'ffn2_wo': ffn2_wo.astype(BF16),
        'mix_norm': row1(mix_norm), 'ple_norm': row1(ple_norm),
        'ple_gate': ple_gate.astype(BF16), 'ple_proj': ple_proj.astype(BF16),
        'ssm_in': ssm_in.astype(BF16), 'ssm_wd': pad_heads(ssm_in[:, :, di + cd:]).astype(BF16),
        'ssm_conv_w': ssm_conv_w, 'ssm_conv_b': row1(ssm_conv_b),
        'ssm_dt_bias': row1(pad_heads(ssm_dt_bias)), 'ssm_a_log': row1(pad_heads(ssm_a_log)),
        'ssm_dsk': row1(jnp.repeat(ssm_d, P, axis=1)),
        'ssm_norm': row1(ssm_norm), 'ssm_out': ssm_out.astype(BF16),
        'kv_norm': kv_norm[None, :], 'w_k': w_kv[:, :dk].astype(BF16), 'w_v': w_kv[:, dk:].astype(BF16),
        'k_norm': jnp.tile(k_norm, KV)[None, :],
        'w_q': w_q.astype(BF16)[:, :, col_perm], 'q_norm': row1(jnp.tile(q_norm, (1, nh))),
        'sinks_row': row1(jnp.pad(sinks_slot, ((0, 0), (0, LANES - nh)))),
        'sinks_col': sinks_slot[:, :, None],
        'w_o': w_o.astype(BF16)[:, col_perm, :],
    }
    bs = x_sample.shape[0]
    y_p, ssm_p, conv_p, k_p, v_p = _trunk(x_prompt, p_prompt, None, None, None, None, 0, w, dims, True)
    y_s, ssm_s, conv_s, k_s, v_s = _trunk(x_sample, p_sample, state_ssm.reshape(n_a, bs, heads * P, N),
                                          state_conv.reshape(n_a, bs, (CONV_TAPS - 1) * cd),
                                          cache_k.reshape(bs, W, dk), cache_v.reshape(bs, W, dk), PAST_LEN, w, dims, False)
    return (y_p, y_s, ssm_p, conv_p, k_p, v_p, ssm_s, conv_s, k_s, v_s)
```

```python
import functools

import numpy as np
import jax
import jax.numpy as jnp
from jax import lax
from jax.experimental import pallas as pl
from jax.experimental.pallas import tpu as pltpu

F32 = jnp.float32
BF16 = jnp.bfloat16
EPS = 1e-6
PAST_LEN = 8192
ROPE_THETA = 10000.0
LANES = 128
SUBLANES = 8
SSD_CHUNK = 128
CONV_TAPS = 4
VMEM_BYTES = 56 * 1024 * 1024


def _dot(a, b):
    return jnp.dot(a, b, preferred_element_type=F32)


def _dot_nt(a, b):
    return lax.dot_general(a, b, (((1,), (1,)), ((), ())), preferred_element_type=F32)


def _split(a, parts):
    pieces = []
    for _ in range(parts):
        piece = a.astype(BF16)
        pieces.append(piece)
        a = a - piece.astype(F32)
    return pieces


def _dot_split_lhs(a, b_stacked, parts=2):
    return _dot(jnp.concatenate(_split(a, parts), axis=1), b_stacked)


def _dot_split_rhs(a_tiled, b, parts=2):
    return _dot(a_tiled, jnp.concatenate(_split(b, parts), axis=0))


def _rms(x, g):
    ms = jnp.mean(x * x, axis=-1, keepdims=True)
    return x * lax.rsqrt(ms + EPS) * g


def _sigmoid(x):
    return 0.5 * jnp.tanh(0.5 * x) + 0.5


def _silu(x):
    half = 0.5 * x
    return half * jnp.tanh(half) + half


def _softplus(v):
    return jnp.maximum(v, 0.0) + jnp.log1p(jnp.exp(-jnp.abs(v)))


def _seg_rms(x, seg_sum, seg_exp, seg):
    ms = _dot((x * x).astype(BF16), seg_sum) * (1.0 / seg)
    return _dot_split_lhs(lax.rsqrt(ms + EPS), seg_exp)


def _rope(x, cos, sin_signed, half):
    width = x.shape[-1]
    lane = lax.broadcasted_iota(jnp.int32, x.shape, 1)
    first = (lane % (2 * half)) < half
    rot = jnp.where(first, pltpu.roll(x, width - half, 1), pltpu.roll(x, half, 1))
    return x * cos + rot * sin_signed


def _const_spec(shape):
    zeros = (0,) * len(shape)
    return pl.BlockSpec(shape, lambda *_: zeros, pipeline_mode=pl.Buffered(1))


def _params(sem):
    return pltpu.CompilerParams(dimension_semantics=sem, vmem_limit_bytes=VMEM_BYTES)


def _seg_mats(width, seg):
    s = np.zeros((width, LANES), np.float32)
    s[np.arange(width), np.arange(width) // seg] = 1.0
    return jnp.asarray(s, BF16), jnp.asarray(np.tile(s.T, (2, 1)), BF16)


def _head_expand_mat(heads, p, parts):
    e = np.zeros((LANES, heads * p), np.float32)
    e[np.arange(heads * p) // p, np.arange(heads * p)] = 1.0
    return jnp.asarray(np.tile(e, (parts, 1)), BF16)


def _half_swiglu(x, g_ref, wi_ref, wo_ref, h_scr, dff, fc, rs=slice(None)):
    xn = _rms(x, g_ref[...]).astype(BF16)
    for c in range(dff // fc):
        gt = _dot(xn, wi_ref[:, c * fc:(c + 1) * fc])
        up = _dot(xn, wi_ref[:, dff + c * fc:dff + (c + 1) * fc])
        h_scr[rs, c * fc:(c + 1) * fc] = (_silu(gt) * up).astype(BF16)
    return x + 0.5 * _dot(h_scr[rs, :], wo_ref[...])


def _normed_rope(v, gain_ref, ssum_ref, sexp_ref, cos_ref, sin_ref, hd, rs):
    vn = v * _seg_rms(v, ssum_ref[...], sexp_ref[...], hd) * gain_ref[...]
    reps = v.shape[1] // LANES
    return _rope(vn, jnp.tile(cos_ref[rs, :], (1, reps)), jnp.tile(sin_ref[rs, :], (1, reps)), hd // 2)


def _layer_spec(arr, layer):
    shape = tuple(arr.shape[1:])
    index = (layer,) + (0,) * len(shape)
    return pl.BlockSpec((None,) + shape, lambda *_: index, pipeline_mode=pl.Buffered(1))


def _pre_kernel(*refs, with_q, dff, fc, hd, scale):
    x_ref, g_ref, wi_ref, wo_ref = refs[:4]
    if with_q:
        mg_ref, wq_ref, qg_ref, ssum_ref, sexp_ref, cos_ref, sin_ref, o_ref, q_ref, h_scr = refs[4:]
    else:
        o_ref, h_scr = refs[4:]
    tm = x_ref.shape[0]
    nsplit = 2 if (with_q and tm % 512 == 0) else 1
    for sp in range(nsplit):
        rs = slice(sp * (tm // nsplit), (sp + 1) * (tm // nsplit))
        x1 = _half_swiglu(x_ref[rs, :], g_ref, wi_ref, wo_ref, h_scr, dff, fc, rs)
        o_ref[rs, :] = x1
        if with_q:
            q = _dot(_rms(x1, mg_ref[...]).astype(BF16), wq_ref[...])
            qr = _normed_rope(q, qg_ref, ssum_ref, sexp_ref, cos_ref, sin_ref, hd, rs)
            q_ref[rs, :] = (qr * scale).astype(q_ref.dtype)


def _pre(x, layer, g, wi, wo, tm, qargs=None, q_dtype=BF16):
    m, d = x.shape
    dff = wo.shape[1]
    fc = 256 if dff % 256 == 0 else dff
    row = lambda w: pl.BlockSpec((tm, w), lambda i: (i, 0))
    in_specs = [row(d), _layer_spec(g, layer), _layer_spec(wi, layer), _layer_spec(wo, layer)]
    args = [x, g, wi, wo]
    out_specs, out_shape = [row(d)], [jax.ShapeDtypeStruct((m, d), F32)]
    hd = 0
    if qargs is not None:
        mg, wq, qg, qlayer, cos, sin, hd = qargs
        dq = wq.shape[2]
        ssum, sexp = _seg_mats(dq, hd)
        nt = cos.shape[0] // tm
        tab = pl.BlockSpec((tm, LANES), lambda i: (i % nt, 0))
        in_specs += [_layer_spec(mg, layer), _layer_spec(wq, qlayer), _layer_spec(qg, qlayer),
                     _const_spec(ssum.shape), _const_spec(sexp.shape), tab, tab]
        args += [mg, wq, qg, ssum, sexp, cos, sin]
        out_specs.append(row(dq))
        out_shape.append(jax.ShapeDtypeStruct((m, dq), q_dtype))
    return pl.pallas_call(
        functools.partial(_pre_kernel, with_q=qargs is not None, dff=dff, fc=fc, hd=hd, scale=hd ** -0.5 if hd else 1.0),
        grid=(m // tm,),
        in_specs=in_specs, out_specs=out_specs, out_shape=out_shape,
        scratch_shapes=[pltpu.VMEM((tm, dff), BF16)],
        compiler_params=_params(("parallel",)),
        name="pre",
    )(*args)


def _post_kernel(*refs, mamba, with_kv, dff, fc, gw, hd):
    it = iter(refs)
    x_ref = next(it)
    if mamba:
        y_ref, gate_ref, ng_ref, ssum_ref, sexp_ref, wmix_ref = [next(it) for _ in range(6)]
    else:
        att_ref, wmix_ref = next(it), next(it)
    g2_ref, wi_ref, wo_ref, pg_ref, wg_ref, p_ref, wp_ref = [next(it) for _ in range(7)]
    if with_kv:
        kvg_ref, wk_ref, wv_ref, kg_ref, ksum_ref, kexp_ref, cos_ref, sin_ref = [next(it) for _ in range(8)]
    o_ref = next(it)
    if with_kv:
        k_ref, v_ref = next(it), next(it)
    h_scr = next(it)

    tm = x_ref.shape[0]
    nsplit = 2 if (mamba and tm % 512 == 0) else 1
    for sp in range(nsplit):
        rs = slice(sp * (tm // nsplit), (sp + 1) * (tm // nsplit))
        if mamba:
            gated = y_ref[rs, :].astype(F32) * gate_ref[rs, :].astype(F32)
            r = _seg_rms(gated, ssum_ref[...], sexp_ref[...], gw)
            mix = _dot((gated * r * ng_ref[...]).astype(BF16), wmix_ref[...])
        else:
            mix = _dot(att_ref[rs, :].astype(BF16), wmix_ref[...])
        x = _half_swiglu(x_ref[rs, :] + mix, g2_ref, wi_ref, wo_ref, h_scr, dff, fc, rs)
        gate = _sigmoid(_dot(_rms(x, pg_ref[...]).astype(BF16), wg_ref[...]))
        x = x + gate * _dot(p_ref[rs, :].astype(BF16), wp_ref[...])
        o_ref[rs, :] = x
        if with_kv:
            xn = _rms(x, kvg_ref[...]).astype(BF16)
            v_ref[rs, :] = _dot(xn, wv_ref[...])
            k_ref[rs, :] = _normed_rope(_dot(xn, wk_ref[...]), kg_ref, ksum_ref, kexp_ref, cos_ref, sin_ref, hd, rs)


def _post(x, layer, mix_args, ffn_args, ple_args, tm, kv_args=None):
    m, d = x.shape
    g2, wi, wo = ffn_args
    pg, wg, p, wp = ple_args
    dff = wo.shape[1]
    fc = 256 if dff % 256 == 0 else dff
    dp = p.shape[2]
    row = lambda w: pl.BlockSpec((tm, w), lambda i: (i, 0))
    in_specs, args = [row(d)], [x]
    mamba = len(mix_args) == 6
    gw = hd = 0
    if mamba:
        y, gate, ng, wout, mlayer, groups = mix_args
        di = y.shape[1]
        gw = di // groups
        ssum, sexp = _seg_mats(di, gw)
        in_specs += [row(di), row(di), _layer_spec(ng, mlayer), _const_spec(ssum.shape), _const_spec(sexp.shape),
                     _layer_spec(wout, mlayer)]
        args += [y, gate, ng, ssum, sexp, wout]
    else:
        att, wmix, mlayer = mix_args
        in_specs += [row(att.shape[1]), _layer_spec(wmix, mlayer)]
        args += [att, wmix]
    in_specs += [_layer_spec(g2, layer), _layer_spec(wi, layer), _layer_spec(wo, layer),
                 _layer_spec(pg, layer), _layer_spec(wg, layer),
                 pl.BlockSpec((None, tm, dp), lambda i: (layer, i, 0)), _layer_spec(wp, layer)]
    args += [g2, wi, wo, pg, wg, p, wp]
    out_specs, out_shape = [row(d)], [jax.ShapeDtypeStruct((m, d), F32)]
    if kv_args is not None:
        kvg, wk, wv, kg, cos, sin, hd = kv_args
        dk = wk.shape[1]
        ksum, kexp = _seg_mats(dk, hd)
        nt = cos.shape[0] // tm
        tab = pl.BlockSpec((tm, LANES), lambda i: (i % nt, 0))
        in_specs += [_const_spec((1, d)), _const_spec((d, dk)), _const_spec((d, dk)), _const_spec((1, dk)),
                     _const_spec(ksum.shape), _const_spec(kexp.shape), tab, tab]
        args += [kvg, wk, wv, kg, ksum, kexp, cos, sin]
        out_specs += [row(dk), row(dk)]
        out_shape += [jax.ShapeDtypeStruct((m, dk), F32)] * 2
    return pl.pallas_call(
        functools.partial(_post_kernel, mamba=mamba, with_kv=kv_args is not None, dff=dff, fc=fc, gw=gw, hd=hd),
        grid=(m // tm,),
        in_specs=in_specs, out_specs=out_specs, out_shape=out_shape,
        scratch_shapes=[pltpu.VMEM((tm, dff), BF16)],
        compiler_params=_params(("parallel",)),
        name="post",
    )(*args)


def _mamba_in_kernel(x_ref, g_ref, w_ref, wd_ref, gate_ref, xbc_ref, dt_ref, *, nc, di):
    xn = _rms(x_ref[...], g_ref[...]).astype(BF16)
    gate_ref[...] = _silu(_dot(xn, w_ref[:, :di])).astype(BF16)
    cw = xbc_ref.shape[1] // nc
    for c in range(nc):
        xbc_ref[:, c * cw:(c + 1) * cw] = _dot(xn, w_ref[:, di + c * cw:di + (c + 1) * cw])
    dt_ref[...] = _dot(xn, wd_ref[...])


def _mamba_in(x, layer, g, w_in, wd, di, cd, tm):
    m, d = x.shape
    row = lambda w: pl.BlockSpec((tm, w), lambda i: (i, 0))
    return pl.pallas_call(
        functools.partial(_mamba_in_kernel, nc=4, di=di),
        grid=(m // tm,),
        in_specs=[row(d), _layer_spec(g, layer), _layer_spec(w_in, layer), _layer_spec(wd, layer)],
        out_specs=[row(di), row(cd), row(LANES)],
        out_shape=[jax.ShapeDtypeStruct((m, di), BF16), jax.ShapeDtypeStruct((m, cd), F32),
                   jax.ShapeDtypeStruct((m, LANES), F32)],
        compiler_params=_params(("parallel",)),
        name="mamba_in",
    )(x, g, w_in, wd)


def _mamba_in_conv_kernel(x_ref, g_ref, w_ref, wd_ref, cw_ref, cb_ref,
                          gate_ref, act_ref, dt_ref, tail_ref, buf, carry, *, tm, L, nc, di):
    t = pl.program_id(1)
    xn = _rms(x_ref[...], g_ref[...]).astype(BF16)
    gate_ref[...] = _silu(_dot(xn, w_ref[:, :di])).astype(BF16)
    dt_ref[...] = _dot(xn, wd_ref[...])
    cw = act_ref.shape[1] // nc

    @pl.when(t == 0)
    def _():
        carry[...] = jnp.zeros_like(carry)

    for c in range(nc):
        cs = slice(c * cw, (c + 1) * cw)
        buf[0:SUBLANES, :] = carry[:, cs]
        buf[SUBLANES:SUBLANES + tm, :] = _dot(xn, w_ref[:, di + c * cw:di + (c + 1) * cw])
        for r in range(tm // L):
            base = SUBLANES + r * L
            conv = cb_ref[:, cs]
            for k in range(CONV_TAPS):
                lo = base - (CONV_TAPS - 1) + k
                conv = conv + cw_ref[k:k + 1, cs] * buf[lo:lo + L, :]
            act_ref[r * L:(r + 1) * L, cs] = _silu(conv).astype(BF16)
        last = buf[tm:tm + SUBLANES, :]
        carry[:, cs] = last
        tail_ref[0, :, cs] = last


def _mamba_in_conv(x, layer, g, w_in, wd, cw, cb, di, cd, bsz, tm):
    m, d = x.shape
    nt = m // bsz // tm
    nc = 4
    row = lambda w: pl.BlockSpec((tm, w), lambda b, t: (b * nt + t, 0))
    return pl.pallas_call(
        functools.partial(_mamba_in_conv_kernel, tm=tm, L=min(SSD_CHUNK, tm), nc=nc, di=di),
        grid=(bsz, nt),
        in_specs=[row(d), _layer_spec(g, layer), _layer_spec(w_in, layer), _layer_spec(wd, layer),
                  _layer_spec(cw, layer), _layer_spec(cb, layer)],
        out_specs=[row(di), row(cd), row(LANES), pl.BlockSpec((1, SUBLANES, cd), lambda b, t: (b, 0, 0))],
        out_shape=[jax.ShapeDtypeStruct((m, di), BF16), jax.ShapeDtypeStruct((m, cd), BF16),
                   jax.ShapeDtypeStruct((m, LANES), F32), jax.ShapeDtypeStruct((bsz, SUBLANES, cd), F32)],
        scratch_shapes=[pltpu.VMEM((tm + SUBLANES, cd // nc), F32), pltpu.VMEM((SUBLANES, cd), F32)],
        compiler_params=_params(("parallel", "arbitrary")),
        name="mamba_in_conv",
    )(x, g, w_in, wd, cw, cb)


def _ssd_kernel(act_ref, dt_ref, dtb_ref, alog_ref, dsk_ref, tril_ref, e_ref,
                y_ref, h_ref, *, tt, L, G, EH, P, N):
    t = pl.program_id(1)
    di = G * EH * P
    gw = EH * P

    @pl.when(t == 0)
    def _():
        h_ref[...] = jnp.zeros_like(h_ref)

    a = -jnp.exp(alog_ref[...])
    tril = tril_ref[...]
    emat = e_ref[...]
    rows = lax.broadcasted_iota(jnp.int32, (L, L), 0)
    cols = lax.broadcasted_iota(jnp.int32, (L, L), 1)
    causal = rows >= cols
    lane = lax.broadcasted_iota(jnp.int32, (L, gw), 1)
    head_masks = [(lane // P) == e for e in range(EH)]

    for c in range(tt // L):
        rs = slice(c * L, (c + 1) * L)
        dt = _softplus(dt_ref[rs, :] + dtb_ref[...])
        cum = _dot_split_rhs(tril, dt * a, parts=3)
        cum_t = cum.T
        dt_t = dt.T
        to_end = jnp.exp(cum[L - 1:L, :] - cum)
        expanded = _dot_split_lhs(jnp.concatenate([dt * to_end, jnp.exp(cum)], axis=0), emat)
        xw_t = (act_ref[rs, :di].astype(F32) * expanded[:L]).T.astype(BF16)
        chunk_decay = jnp.broadcast_to(jnp.exp(cum_t[:, L - 1:L]), (LANES, N))

        for g in range(G):
            bg = act_ref[rs, di + g * N:di + (g + 1) * N]
            cg = act_ref[rs, di + (G + g) * N:di + (G + g + 1) * N]
            cb = _dot_nt(cg, bg)
            sl = slice(g * gw, (g + 1) * gw)
            xs_g = act_ref[rs, sl]
            mats, blocks = [], []
            for e in range(EH):
                h = g * EH + e
                seg = cum[:, h:h + 1] - cum_t[h:h + 1, :]
                decay = jnp.exp(jnp.where(causal, seg, -jnp.inf))
                mats.append((cb * decay * dt_t[h:h + 1, :]).astype(BF16))
                blocks.append(jnp.where(head_masks[e], xs_g, jnp.zeros_like(xs_g)))
            y_intra = _dot(jnp.concatenate(mats, axis=1), jnp.concatenate(blocks, axis=0))
            hg = h_ref[0, sl, :]
            y_inter = _dot_nt(cg, hg.astype(BF16)) * expanded[L:, sl]
            y_ref[rs, sl] = (y_intra + y_inter + dsk_ref[:, sl] * xs_g.astype(F32)).astype(y_ref.dtype)
            s_new = _dot(xw_t[sl, :], bg)
            for e in range(EH):
                h = g * EH + e
                hs = slice(e * P, (e + 1) * P)
                h_ref[0, g * gw + e * P:g * gw + (e + 1) * P, :] = hg[hs, :] * chunk_decay[h:h + 1, :] + s_new[hs, :]


def _ssd(act, dt_raw, layer, dtb, alog, dsk, bsz, G, EH, P, N, tt):
    m, cd = act.shape
    nt = m // bsz // tt
    L = min(SSD_CHUNK, tt)
    heads = G * EH
    di = heads * P
    tril = jnp.asarray(np.tile(np.tril(np.ones((L, L), np.float32)), (1, 3)), BF16)
    emat = _head_expand_mat(heads, P, 2)
    row = lambda w: pl.BlockSpec((tt, w), lambda b, t: (b * nt + t, 0))
    return pl.pallas_call(
        functools.partial(_ssd_kernel, tt=tt, L=L, G=G, EH=EH, P=P, N=N),
        grid=(bsz, nt),
        in_specs=[row(cd), row(LANES), _layer_spec(dtb, layer), _layer_spec(alog, layer), _layer_spec(dsk, layer),
                  _const_spec(tril.shape), _const_spec(emat.shape)],
        out_specs=[row(di), pl.BlockSpec((1, di, N), lambda b, t: (b, 0, 0))],
        out_shape=[jax.ShapeDtypeStruct((m, di), BF16), jax.ShapeDtypeStruct((bsz, di, N), F32)],
        compiler_params=_params(("parallel", "arbitrary")),
        name="ssd",
    )(act, dt_raw, dtb, alog, dsk, tril, emat)


def _ssm_step_kernel(*refs, bb, G, EH, P, N, layer, aliased):
    (h0_ref, xbc_ref, c0_ref, dt_ref, cw_ref, cb_ref, dtb_ref, alog_ref, dsk_ref, e_ref) = refs[:10]
    rest = refs[11:] if aliased else refs[10:]
    hn_all, y_ref, cnew_ref, xdt_t_scr, dec_t_scr, b_scr, c_scr, xs_scr = rest
    if aliased:
        hn_ref = hn_all
    else:
        hn_ref = hn_all.at[layer]
        for other in range(hn_all.shape[0]):
            if other != layer:
                hn_all[other] = jnp.zeros(hn_all.shape[1:], F32)
    i = pl.program_id(0)
    cd = xbc_ref.shape[1]
    di = G * EH * P
    gw = EH * P
    db = xbc_ref.shape[0]

    @pl.when(i == 0)
    def _():
        x = xbc_ref[...]
        conv = cb_ref[...] + cw_ref[CONV_TAPS - 1:CONV_TAPS, :] * x
        for k in range(CONV_TAPS - 1):
            conv = conv + cw_ref[k:k + 1, :] * c0_ref[:, k * cd:(k + 1) * cd]
        cnew_ref[:, 0:(CONV_TAPS - 2) * cd] = c0_ref[:, cd:(CONV_TAPS - 1) * cd]
        cnew_ref[:, (CONV_TAPS - 2) * cd:(CONV_TAPS - 1) * cd] = x
        act = _silu(conv)
        xs = act[:, :di]
        xs_scr[...] = xs
        b_scr[...] = act[:, di:di + G * N]
        c_scr[...] = act[:, di + G * N:]
        dt = _softplus(dt_ref[...] + dtb_ref[...])
        decay = jnp.exp(dt * (-jnp.exp(alog_ref[...])))
        xdt_t_scr[...] = (xs * _dot_split_lhs(dt, e_ref[...], parts=3)).T
        dec_t_scr[...] = decay.T

    seq = lax.broadcasted_iota(jnp.int32, (1, db), 1)
    for j in range(bb):
        b = i * bb + j
        pick = seq == b
        xcol = jnp.sum(jnp.where(pick, xdt_t_scr[...], 0.0), axis=1, keepdims=True)
        dcol = jnp.sum(jnp.where(pick, dec_t_scr[...], 0.0), axis=1, keepdims=True)
        brow = b_scr[pl.ds(b, 1), :]
        crow = c_scr[pl.ds(b, 1), :]
        parts = []
        for g in range(G):
            bg = brow[:, g * N:(g + 1) * N]
            pieces = []
            for e in range(EH):
                h = g * EH + e
                rows = slice(h * P, (h + 1) * P)
                pieces.append(h0_ref[j, rows, :] * dcol[h:h + 1, :] + xcol[rows, :] * bg)
            hn = jnp.concatenate(pieces, axis=0)
            hn_ref[j, g * gw:(g + 1) * gw, :] = hn
            c8 = jnp.broadcast_to(crow[:, g * N:(g + 1) * N], (SUBLANES, N)).astype(BF16)
            parts.append(_dot_nt(c8, hn.astype(BF16))[0:1, :])
        y_ref[pl.ds(b, 1), :] = jnp.concatenate(parts, axis=1) + dsk_ref[...] * xs_scr[pl.ds(b, 1), :]


def _ssm_step(h_all, layer, h_prev_out, xbc, c0, dt_raw, cw, cb, dtb, alog, dsk, G, EH, P, N, bb):
    db, cd = xbc.shape
    hp = G * EH * P
    di = hp
    emat = _head_expand_mat(G * EH, P, 3)
    st = pl.BlockSpec((None, bb, hp, N), lambda i: (layer, i, 0, 0))
    st_all = pl.BlockSpec((h_all.shape[0], bb, hp, N), lambda i: (0, i, 0, 0))
    taps = CONV_TAPS - 1
    in_specs = [st, _const_spec((db, cd)), _layer_spec(c0, layer), _const_spec((db, LANES)),
                _layer_spec(cw, layer), _layer_spec(cb, layer), _layer_spec(dtb, layer), _layer_spec(alog, layer),
                _layer_spec(dsk, layer), _const_spec(emat.shape)]
    args = [h_all, xbc, c0, dt_raw, cw, cb, dtb, alog, dsk, emat]
    aliases = {}
    if h_prev_out is not None:
        in_specs.append(pl.BlockSpec(memory_space=pl.ANY))
        args.append(h_prev_out)
        aliases = {len(args) - 1: 0}
    return pl.pallas_call(
        functools.partial(_ssm_step_kernel, bb=bb, G=G, EH=EH, P=P, N=N, layer=layer, aliased=h_prev_out is not None),
        grid=(db // bb,),
        in_specs=in_specs,
        out_specs=[st if h_prev_out is not None else st_all,
                   pl.BlockSpec((db, di), lambda i: (0, 0)), pl.BlockSpec((db, taps * cd), lambda i: (0, 0))],
        out_shape=[jax.ShapeDtypeStruct(h_all.shape, F32), jax.ShapeDtypeStruct((db, di), F32),
                   jax.ShapeDtypeStruct((db, taps * cd), F32)],
        scratch_shapes=[pltpu.VMEM((hp, db), F32), pltpu.VMEM((LANES, db), F32),
                        pltpu.VMEM((db, G * N), F32), pltpu.VMEM((db, G * N), F32), pltpu.VMEM((db, di), F32)],
        input_output_aliases=aliases,
        compiler_params=_params(("arbitrary",)),
        name="ssm_step",
    )(*args)


def _attn_kernel(q_ref, kh_ref, km_ref, vh_ref, vm_ref, sink_ref, o_ref, *, tq, W, KV, GQ, hd):
    t = pl.program_id(1)
    per = LANES // hd
    key = lax.broadcasted_iota(jnp.int32, (W, W), 0)
    r = lax.broadcasted_iota(jnp.int32, (W, W), 1)
    cur_ok = key <= r
    half_k = lax.broadcasted_iota(jnp.int32, (2 * W, LANES), 1) // hd
    half_o = lax.broadcasted_iota(jnp.int32, (LANES, W), 0) // hd
    nblk = tq // W
    for j in range(nblk):
        first = jnp.logical_and(t == 0, j == 0)
        prev_ok = key >= jnp.maximum(r, jnp.where(first, W, 0))
        rows_cur = slice(j * W, (j + 1) * W)
        if j == 0:
            k2 = jnp.concatenate([kh_ref[...], km_ref[rows_cur, :]], axis=0)
            v2 = jnp.concatenate([vh_ref[...], vm_ref[rows_cur, :]], axis=0)
        else:
            k2 = km_ref[(j - 1) * W:(j + 1) * W, :]
            v2 = vm_ref[(j - 1) * W:(j + 1) * W, :]
        for blk in range(KV // per):
            ls = slice(blk * LANES, (blk + 1) * LANES)
            kblk = k2[:, ls]
            k_u = [jnp.where(half_k == u, kblk, 0.0).astype(BF16) for u in range(per)]
            v_t = v2[:, ls].T.astype(BF16)
            for e in range(GQ):
                cb = (e * (KV // per) + blk) * LANES
                qs = q_ref[j * W:(j + 1) * W, cb:cb + LANES]
                acc_t = None
                for u in range(per):
                    slot = e * KV + blk * per + u
                    sink = sink_ref[:, slot:slot + 1]
                    s_t = _dot_nt(k_u[u], qs)
                    s_prev = jnp.where(prev_ok, s_t[:W], -jnp.inf)
                    s_cur = jnp.where(cur_ok, s_t[W:], -jnp.inf)
                    mx = jnp.maximum(jnp.max(jnp.maximum(s_prev, s_cur), axis=0, keepdims=True), sink)
                    p_prev = jnp.exp(s_prev - mx)
                    p_cur = jnp.exp(s_cur - mx)
                    den = jnp.sum(p_prev + p_cur, axis=0, keepdims=True) + jnp.exp(sink - mx)
                    o_t = _dot(v_t, jnp.concatenate([p_prev, p_cur], axis=0).astype(BF16)) / den
                    acc_t = o_t if acc_t is None else jnp.where(half_o == u, o_t, acc_t)
                o_ref[j * W:(j + 1) * W, cb:cb + LANES] = acc_t.T.astype(o_ref.dtype)


def _attn(q, k, v, sinks, layer, bsz, W, KV, GQ, hd, tq):
    m, dq = q.shape
    nt = m // bsz // tq
    dk = KV * hd
    per_tile = tq // W
    row = lambda w: pl.BlockSpec((tq, w), lambda b, t: (b * nt + t, 0))
    head = pl.BlockSpec((W, dk), lambda b, t: (jnp.maximum((b * nt + t) * per_tile - 1, 0), 0))
    return pl.pallas_call(
        functools.partial(_attn_kernel, tq=tq, W=W, KV=KV, GQ=GQ, hd=hd),
        grid=(bsz, nt),
        in_specs=[row(dq), head, row(dk), head, row(dk), _layer_spec(sinks, layer)],
        out_specs=row(dq),
        out_shape=jax.ShapeDtypeStruct((m, dq), BF16),
        compiler_params=_params(("parallel", "parallel")),
        name="attn",
    )(q, k, k, v, v, sinks)


def _attn_step_kernel(q_ref, kc_ref, vc_ref, kn_ref, vn_ref, sink_ref, o_ref, *, bb, KV, GQ, hd):
    i = pl.program_id(0)
    nh = KV * GQ
    dq = nh * hd
    dk = KV * hd
    lane_q = lax.broadcasted_iota(jnp.int32, (nh, dq), 1)
    row_q = lax.broadcasted_iota(jnp.int32, (nh, dq), 0)
    own_q = (lane_q // hd) == row_q
    lane_k = lax.broadcasted_iota(jnp.int32, (nh, dk), 1)
    row_k = lax.broadcasted_iota(jnp.int32, (nh, dk), 0)
    own_k = (lane_k // hd) == (row_k % KV)
    sink = sink_ref[...]
    for j in range(bb):
        b = i * bb + j
        qm = jnp.where(own_q, jnp.broadcast_to(q_ref[pl.ds(b, 1), :], (nh, dq)), 0.0)
        qbd = qm[:, 0:dk]
        for e in range(1, GQ):
            qbd = qbd + qm[:, e * dk:(e + 1) * dk]
        s = _dot_nt(qbd.astype(BF16), kc_ref[j].astype(BF16))
        s_new = jnp.sum(qbd * kn_ref[pl.ds(b, 1), :], axis=-1, keepdims=True)
        mx = jnp.maximum(jnp.maximum(jnp.max(s, axis=-1, keepdims=True), s_new), sink)
        p = jnp.exp(s - mx)
        p_new = jnp.exp(s_new - mx)
        den = jnp.sum(p, axis=-1, keepdims=True) + p_new + jnp.exp(sink - mx)
        o = (_dot(p.astype(BF16), vc_ref[j].astype(BF16)) + p_new * vn_ref[pl.ds(b, 1), :]) / den
        om = jnp.where(own_k, o, 0.0)
        pieces = [jnp.sum(om[e * KV:(e + 1) * KV, :], axis=0, keepdims=True) for e in range(GQ)]
        o_ref[pl.ds(b, 1), :] = jnp.concatenate(pieces, axis=1)


def _attn_step(q, kc, vc, kn, vn, sinks_col, layer, KV, GQ, hd, bb):
    db, dq = q.shape
    W, dk = kc.shape[1], kc.shape[2]
    cache = pl.BlockSpec((bb, W, dk), lambda i: (i, 0, 0))
    return pl.pallas_call(
        functools.partial(_attn_step_kernel, bb=bb, KV=KV, GQ=GQ, hd=hd),
        grid=(db // bb,),
        in_specs=[_const_spec((db, dq)), cache, cache, _const_spec((db, dk)), _const_spec((db, dk)),
                  _layer_spec(sinks_col, layer)],
        out_specs=pl.BlockSpec((db, dq), lambda i: (0, 0)),
        out_shape=jax.ShapeDtypeStruct((db, dq), F32),
        compiler_params=_params(("arbitrary",)),
        name="attn_step",
    )(q, kc, vc, kn, vn, sinks_col)


def _rope_tables(pos, hd):
    half = hd // 2
    inv = ROPE_THETA ** (-jnp.arange(half, dtype=F32) / half)
    ang = pos.astype(F32)[:, None] * inv[None, :]
    cos, sin = jnp.cos(ang), jnp.sin(ang)
    reps = LANES // hd
    return (jnp.tile(jnp.concatenate([cos, cos], axis=1), (1, reps)),
            jnp.tile(jnp.concatenate([-sin, sin], axis=1), (1, reps)))


def _trunk(x, p, ssm0, conv0, k_hist, v_hist, pos0, w, dims, is_prompt):
    bsz, t_len, d = x.shape
    m = bsz * t_len
    depth, n_a, G, EH, P, N, KV, GQ, hd, W = dims
    di = G * EH * P
    cd = di + 2 * G * N
    dk = KV * hd
    tm = 512 if t_len % 512 == 0 else m
    x = x.reshape(m, d)
    p = p.reshape(depth, m, -1)
    pos = pos0 + (jnp.arange(t_len) if is_prompt else jnp.zeros((m,), jnp.int32))
    cos, sin = _rope_tables(pos, hd)
    ffn1 = (w['ffn1_norm'], w['ffn1_wi'], w['ffn1_wo'])
    new_conv = []
    ssm_out = None
    k_new = v_new = None
    mix_args = None
    for i in range(depth):
        if i < n_a:
            win = (w['mix_norm'], w['ssm_in'], w['ssm_wd'])
            sargs = (w['ssm_dt_bias'], w['ssm_a_log'], w['ssm_dsk'])
            x = _pre(x, i, *ffn1, tm)[0]
            if is_prompt:
                gate, act, dt_raw, tail = _mamba_in_conv(x, i, *win, w['ssm_conv_w'], w['ssm_conv_b'], di, cd, bsz, tm)
                y, h_last = _ssd(act, dt_raw, i, *sargs, bsz, G, EH, P, N, tt=min(512, t_len))
                ssm_out = h_last[None] if ssm_out is None else jnp.concatenate([ssm_out, h_last[None]], axis=0)
                new_conv.append(tail[:, SUBLANES - (CONV_TAPS - 1):])
            else:
                gate, xbc, dt_raw = _mamba_in(x, i, *win, di, cd, tm)
                ssm_out, y, c_last = _ssm_step(ssm0, i, ssm_out, xbc, conv0, dt_raw,
                                               w['ssm_conv_w'], w['ssm_conv_b'], *sargs, G, EH, P, N, bb=4)
                new_conv.append(c_last.reshape(bsz, CONV_TAPS - 1, cd))
            mix_args = (y, gate, w['ssm_norm'], w['ssm_out'], i, G)
        else:
            j = i - n_a
            qargs = (w['mix_norm'], w['w_q'], w['q_norm'], j, cos, sin, hd)
            x, q = _pre(x, i, *ffn1, tm, qargs, q_dtype=BF16 if is_prompt else F32)
            if is_prompt:
                o = _attn(q, k_new, v_new, w['sinks_row'], j, bsz, W, KV, GQ, hd, tq=min(512, t_len))
            else:
                o = _attn_step(q, k_hist, v_hist, k_new, v_new, w['sinks_col'], j, KV, GQ, hd, bb=8)
            mix_args = (o, w['w_o'], j)
        kv_args = (w['kv_norm'], w['w_k'], w['w_v'], w['k_norm'], cos, sin, hd) if i == n_a - 1 else None
        res = _post(x, i, mix_args, (w['ffn2_norm'], w['ffn2_wi'], w['ffn2_wo']),
                    (w['ple_norm'], w['ple_gate'], p, w['ple_proj']), tm, kv_args)
        x = res[0]
        if kv_args is not None:
            k_new, v_new = res[1], res[2]
    if is_prompt:
        k_out = k_new.reshape(bsz, t_len, dk)[:, t_len - W:]
        v_out = v_new.reshape(bsz, t_len, dk)[:, t_len - W:]
    else:
        k_out = jnp.concatenate([k_hist, k_new.reshape(bsz, t_len, dk)], axis=1)[:, -W:]
        v_out = jnp.concatenate([v_hist, v_new.reshape(bsz, t_len, dk)], axis=1)[:, -W:]
    return (x.reshape(bsz, t_len, d), ssm_out.reshape(n_a, bsz, G * EH, P, N), jnp.stack(new_conv),
            k_out.reshape(bsz, W, KV, hd), v_out.reshape(bsz, W, KV, hd))


def kernel(x_prompt, x_sample, state_ssm, state_conv, cache_k, cache_v, p_prompt, p_sample, ffn1_norm, ffn1_wi, ffn1_wo, mix_norm, ffn2_norm, ffn2_wi, ffn2_wo, ple_norm, ple_gate, ple_proj, ssm_in, ssm_conv_w, ssm_conv_b, ssm_dt_bias, ssm_a_log, ssm_d, ssm_norm, ssm_out, kv_norm, w_kv, k_norm, w_q, q_norm, attn_sinks, w_o):
    depth, d = ffn1_norm.shape
    n_a, heads = ssm_a_log.shape
    di = ssm_norm.shape[1]
    cd = ssm_conv_b.shape[1]
    N = state_ssm.shape[-1]
    P = state_ssm.shape[-2]
    G = (cd - di) // (2 * N)
    EH = heads // G
    W, KV, hd = cache_k.shape[1], cache_k.shape[2], cache_k.shape[3]
    nh = w_q.shape[2] // hd
    GQ = nh // KV
    dims = (depth, n_a, G, EH, P, N, KV, GQ, hd, W)

    nb, d_q = w_q.shape[0], w_q.shape[1]
    wq_slot = jnp.swapaxes(w_q.astype(BF16).reshape(nb, d_q, KV, GQ, hd), 2, 3).reshape(nb, d_q, nh * hd)
    wo_slot = jnp.swapaxes(w_o.astype(BF16).reshape(nb, KV, GQ, hd, w_o.shape[2]), 1, 2).reshape(nb, nh * hd, w_o.shape[2])
    sinks_slot = jnp.swapaxes(attn_sinks.reshape(nb, KV, GQ), 1, 2).reshape(nb, nh)
    row1 = lambda v: v[:, None, :]
    pad_heads = lambda v: jnp.pad(v, ((0, 0),) * (v.ndim - 1) + ((0, LANES - heads),))
    dk = KV * hd
    w = {
        'ffn1_norm': row1(ffn1_norm), 'ffn1_wi': ffn1_wi.astype(BF16), 'ffn1_wo': ffn1_wo.astype(BF16),
        'ffn2_norm': row1(ffn2_norm), 'ffn2_wi': ffn2_wi.astype(BF16), 'ffn2_wo': ffn2_wo.astype(BF16),
        'mix_norm': row1(mix_norm), 'ple_norm': row1(ple_norm),
        'ple_gate': ple_gate.astype(BF16), 'ple_proj': ple_proj.astype(BF16),
        'ssm_in': ssm_in.astype(BF16), 'ssm_wd': pad_heads(ssm_in[:, :, di + cd:]).astype(BF16),
        'ssm_conv_w': ssm_conv_w, 'ssm_conv_b': row1(ssm_conv_b),
        'ssm_dt_bias': row1(pad_heads(ssm_dt_bias)), 'ssm_a_log': row1(pad_heads(ssm_a_log)),
        'ssm_dsk': row1(jnp.repeat(ssm_d, P, axis=1)),
        'ssm_norm': row1(ssm_norm), 'ssm_out': ssm_out.astype(BF16),
        'kv_norm': kv_norm[None, :], 'w_k': w_kv[:, :dk].astype(BF16), 'w_v': w_kv[:, dk:].astype(BF16),
        'k_norm': jnp.tile(k_norm, KV)[None, :],
        'w_q': wq_slot, 'q_norm': row1(jnp.tile(q_norm, (1, nh))),
        'sinks_row': row1(jnp.pad(sinks_slot, ((0, 0), (0, LANES - nh)))),
        'sinks_col': sinks_slot[:, :, None],
        'w_o': wo_slot,
    }
    bs = x_sample.shape[0]
    y_p, ssm_p, conv_p, k_p, v_p = _trunk(x_prompt, p_prompt, None, None, None, None, 0, w, dims, True)
    y_s, ssm_s, conv_s, k_s, v_s = _trunk(x_sample, p_sample, state_ssm.reshape(n_a, bs, heads * P, N),
                                          state_conv.reshape(n_a, bs, (CONV_TAPS - 1) * cd),
                                          cache_k.reshape(bs, W, dk), cache_v.reshape(bs, W, dk), PAST_LEN, w, dims, False)
    return (y_p, y_s, ssm_p, conv_p, k_p, v_p, ssm_s, conv_s, k_s, v_s)
```

```python
import functools

import numpy as np
import jax
import jax.numpy as jnp
from jax import lax
from jax.experimental import pallas as pl
from jax.experimental.pallas import tpu as pltpu

F32 = jnp.float32
BF16 = jnp.bfloat16
EPS = 1e-6
PAST_LEN = 8192
ROPE_THETA = 10000.0
LANES = 128
SUBLANES = 8
SSD_CHUNK = 128
CONV_TAPS = 4
VMEM_BYTES = 56 * 1024 * 1024


def _dot(a, b):
    return jnp.dot(a, b, preferred_element_type=F32)


def _dot_nt(a, b):
    return lax.dot_general(a, b, (((1,), (1,)), ((), ())), preferred_element_type=F32)


def _split(a, parts):
    pieces = []
    for _ in range(parts):
        piece = a.astype(BF16)
        pieces.append(piece)
        a = a - piece.astype(F32)
    return pieces


def _dot_split_lhs(a, b_stacked, parts=2):
    return _dot(jnp.concatenate(_split(a, parts), axis=1), b_stacked)


def _dot_split_rhs(a_tiled, b, parts=2):
    return _dot(a_tiled, jnp.concatenate(_split(b, parts), axis=0))


def _rms(x, g):
    ms = jnp.mean(x * x, axis=-1, keepdims=True)
    return x * lax.rsqrt(ms + EPS) * g


def _sigmoid(x):
    return 0.5 * jnp.tanh(0.5 * x) + 0.5


def _silu(x):
    half = 0.5 * x
    return half * jnp.tanh(half) + half


def _softplus(v):
    return jnp.maximum(v, 0.0) + jnp.log1p(jnp.exp(-jnp.abs(v)))


def _seg_rms(x, seg_sum, seg_exp, seg):
    ms = _dot((x * x).astype(BF16), seg_sum) * (1.0 / seg)
    return _dot_split_lhs(lax.rsqrt(ms + EPS), seg_exp)


def _rope(x, cos, sin_signed, half):
    width = x.shape[-1]
    lane = lax.broadcasted_iota(jnp.int32, x.shape, 1)
    first = (lane % (2 * half)) < half
    rot = jnp.where(first, pltpu.roll(x, width - half, 1), pltpu.roll(x, half, 1))
    return x * cos + rot * sin_signed


def _const_spec(shape):
    zeros = (0,) * len(shape)
    return pl.BlockSpec(shape, lambda *_: zeros, pipeline_mode=pl.Buffered(1))


def _params(sem):
    return pltpu.CompilerParams(dimension_semantics=sem, vmem_limit_bytes=VMEM_BYTES)


def _seg_mats(width, seg):
    s = np.zeros((width, LANES), np.float32)
    s[np.arange(width), np.arange(width) // seg] = 1.0
    return jnp.asarray(s, BF16), jnp.asarray(np.tile(s.T, (2, 1)), BF16)


def _head_expand_mat(heads, p, parts):
    e = np.zeros((LANES, heads * p), np.float32)
    e[np.arange(heads * p) // p, np.arange(heads * p)] = 1.0
    return jnp.asarray(np.tile(e, (parts, 1)), BF16)


def _half_swiglu(x, g_ref, wi_ref, wo_ref, h_scr, dff, fc, rs=slice(None)):
    xn = _rms(x, g_ref[...]).astype(BF16)
    for c in range(dff // fc):
        gt = _dot(xn, wi_ref[:, c * fc:(c + 1) * fc])
        up = _dot(xn, wi_ref[:, dff + c * fc:dff + (c + 1) * fc])
        h_scr[rs, c * fc:(c + 1) * fc] = (_silu(gt) * up).astype(BF16)
    return x + 0.5 * _dot(h_scr[rs, :], wo_ref[...])


def _normed_rope(v, gain_ref, ssum_ref, sexp_ref, cos_ref, sin_ref, hd, rs):
    vn = v * _seg_rms(v, ssum_ref[...], sexp_ref[...], hd) * gain_ref[...]
    reps = v.shape[1] // LANES
    return _rope(vn, jnp.tile(cos_ref[rs, :], (1, reps)), jnp.tile(sin_ref[rs, :], (1, reps)), hd // 2)


def _layer_spec(arr, layer):
    shape = tuple(arr.shape[1:])
    index = (layer,) + (0,) * len(shape)
    return pl.BlockSpec((None,) + shape, lambda *_: index, pipeline_mode=pl.Buffered(1))


def _pre_kernel(*refs, with_q, dff, fc, hd, scale):
    x_ref, g_ref, wi_ref, wo_ref = refs[:4]
    if with_q:
        mg_ref, wq_ref, qg_ref, ssum_ref, sexp_ref, cos_ref, sin_ref, o_ref, q_ref, h_scr = refs[4:]
    else:
        o_ref, h_scr = refs[4:]
    tm = x_ref.shape[0]
    nsplit = 2 if (with_q and tm % 512 == 0) else 1
    for sp in range(nsplit):
        rs = slice(sp * (tm // nsplit), (sp + 1) * (tm // nsplit))
        x1 = _half_swiglu(x_ref[rs, :], g_ref, wi_ref, wo_ref, h_scr, dff, fc, rs)
        o_ref[rs, :] = x1
        if with_q:
            q = _dot(_rms(x1, mg_ref[...]).astype(BF16), wq_ref[...])
            qr = _normed_rope(q, qg_ref, ssum_ref, sexp_ref, cos_ref, sin_ref, hd, rs)
            q_ref[rs, :] = (qr * scale).astype(q_ref.dtype)


def _pre(x, layer, g, wi, wo, tm, qargs=None, q_dtype=BF16):
    m, d = x.shape
    dff = wo.shape[1]
    fc = 256 if dff % 256 == 0 else dff
    row = lambda w: pl.BlockSpec((tm, w), lambda i: (i, 0))
    in_specs = [row(d), _layer_spec(g, layer), _layer_spec(wi, layer), _layer_spec(wo, layer)]
    args = [x, g, wi, wo]
    out_specs, out_shape = [row(d)], [jax.ShapeDtypeStruct((m, d), F32)]
    hd = 0
    if qargs is not None:
        mg, wq, qg, qlayer, cos, sin, hd = qargs
        dq = wq.shape[2]
        ssum, sexp = _seg_mats(dq, hd)
        nt = cos.shape[0] // tm
        tab = pl.BlockSpec((tm, LANES), lambda i: (i % nt, 0))
        in_specs += [_layer_spec(mg, layer), _layer_spec(wq, qlayer), _layer_spec(qg, qlayer),
                     _const_spec(ssum.shape), _const_spec(sexp.shape), tab, tab]
        args += [mg, wq, qg, ssum, sexp, cos, sin]
        out_specs.append(row(dq))
        out_shape.append(jax.ShapeDtypeStruct((m, dq), q_dtype))
    return pl.pallas_call(
        functools.partial(_pre_kernel, with_q=qargs is not None, dff=dff, fc=fc, hd=hd, scale=hd ** -0.5 if hd else 1.0),
        grid=(m // tm,),
        in_specs=in_specs, out_specs=out_specs, out_shape=out_shape,
        scratch_shapes=[pltpu.VMEM((tm, dff), BF16)],
        compiler_params=_params(("parallel",)),
        name="pre",
    )(*args)


def _post_kernel(*refs, mamba, with_kv, dff, fc, gw, hd):
    it = iter(refs)
    x_ref = next(it)
    if mamba:
        y_ref, gate_ref, ng_ref, ssum_ref, sexp_ref, wmix_ref = [next(it) for _ in range(6)]
    else:
        att_ref, wmix_ref = next(it), next(it)
    g2_ref, wi_ref, wo_ref, pg_ref, wg_ref, p_ref, wp_ref = [next(it) for _ in range(7)]
    if with_kv:
        kvg_ref, wk_ref, wv_ref, kg_ref, ksum_ref, kexp_ref, cos_ref, sin_ref = [next(it) for _ in range(8)]
    o_ref = next(it)
    if with_kv:
        k_ref, v_ref = next(it), next(it)
    h_scr = next(it)

    tm = x_ref.shape[0]
    nsplit = 2 if (mamba and tm % 512 == 0) else 1
    for sp in range(nsplit):
        rs = slice(sp * (tm // nsplit), (sp + 1) * (tm // nsplit))
        if mamba:
            gated = y_ref[rs, :].astype(F32) * gate_ref[rs, :].astype(F32)
            r = _seg_rms(gated, ssum_ref[...], sexp_ref[...], gw)
            mix = _dot((gated * r * ng_ref[...]).astype(BF16), wmix_ref[...])
        else:
            mix = _dot(att_ref[rs, :].astype(BF16), wmix_ref[...])
        x = _half_swiglu(x_ref[rs, :] + mix, g2_ref, wi_ref, wo_ref, h_scr, dff, fc, rs)
        gate = _sigmoid(_dot(_rms(x, pg_ref[...]).astype(BF16), wg_ref[...]))
        x = x + gate * _dot(p_ref[rs, :].astype(BF16), wp_ref[...])
        o_ref[rs, :] = x
        if with_kv:
            xn = _rms(x, kvg_ref[...]).astype(BF16)
            v_ref[rs, :] = _dot(xn, wv_ref[...])
            k_ref[rs, :] = _normed_rope(_dot(xn, wk_ref[...]), kg_ref, ksum_ref, kexp_ref, cos_ref, sin_ref, hd, rs)


def _post(x, layer, mix_args, ffn_args, ple_args, tm, kv_args=None):
    m, d = x.shape
    g2, wi, wo = ffn_args
    pg, wg, p, wp = ple_args
    dff = wo.shape[1]
    fc = 256 if dff % 256 == 0 else dff
    dp = p.shape[2]
    row = lambda w: pl.BlockSpec((tm, w), lambda i: (i, 0))
    in_specs, args = [row(d)], [x]
    mamba = len(mix_args) == 6
    gw = hd = 0
    if mamba:
        y, gate, ng, wout, mlayer, groups = mix_args
        di = y.shape[1]
        gw = di // groups
        ssum, sexp = _seg_mats(di, gw)
        in_specs += [row(di), row(di), _layer_spec(ng, mlayer), _const_spec(ssum.shape), _const_spec(sexp.shape),
                     _layer_spec(wout, mlayer)]
        args += [y, gate, ng, ssum, sexp, wout]
    else:
        att, wmix, mlayer = mix_args
        in_specs += [row(att.shape[1]), _layer_spec(wmix, mlayer)]
        args += [att, wmix]
    in_specs += [_layer_spec(g2, layer), _layer_spec(wi, layer), _layer_spec(wo, layer),
                 _layer_spec(pg, layer), _layer_spec(wg, layer),
                 pl.BlockSpec((None, tm, dp), lambda i: (layer, i, 0)), _layer_spec(wp, layer)]
    args += [g2, wi, wo, pg, wg, p, wp]
    out_specs, out_shape = [row(d)], [jax.ShapeDtypeStruct((m, d), F32)]
    if kv_args is not None:
        kvg, wk, wv, kg, cos, sin, hd = kv_args
        dk = wk.shape[1]
        ksum, kexp = _seg_mats(dk, hd)
        nt = cos.shape[0] // tm
        tab = pl.BlockSpec((tm, LANES), lambda i: (i % nt, 0))
        in_specs += [_const_spec((1, d)), _const_spec((d, dk)), _const_spec((d, dk)), _const_spec((1, dk)),
                     _const_spec(ksum.shape), _const_spec(kexp.shape), tab, tab]
        args += [kvg, wk, wv, kg, ksum, kexp, cos, sin]
        out_specs += [row(dk), row(dk)]
        out_shape += [jax.ShapeDtypeStruct((m, dk), F32)] * 2
    return pl.pallas_call(
        functools.partial(_post_kernel, mamba=mamba, with_kv=kv_args is not None, dff=dff, fc=fc, gw=gw, hd=hd),
        grid=(m // tm,),
        in_specs=in_specs, out_specs=out_specs, out_shape=out_shape,
        scratch_shapes=[pltpu.VMEM((tm, dff), BF16)],
        compiler_params=_params(("parallel",)),
        name="post",
    )(*args)


def _mamba_in_kernel(x_ref, g_ref, w_ref, wd_ref, gate_ref, xbc_ref, dt_ref, *, nc, di):
    xn = _rms(x_ref[...], g_ref[...]).astype(BF16)
    gate_ref[...] = _silu(_dot(xn, w_ref[:, :di])).astype(BF16)
    cw = xbc_ref.shape[1] // nc
    for c in range(nc):
        xbc_ref[:, c * cw:(c + 1) * cw] = _dot(xn, w_ref[:, di + c * cw:di + (c + 1) * cw])
    dt_ref[...] = _dot(xn, wd_ref[...])


def _mamba_in(x, layer, g, w_in, wd, di, cd, tm):
    m, d = x.shape
    row = lambda w: pl.BlockSpec((tm, w), lambda i: (i, 0))
    return pl.pallas_call(
        functools.partial(_mamba_in_kernel, nc=4, di=di),
        grid=(m // tm,),
        in_specs=[row(d), _layer_spec(g, layer), _layer_spec(w_in, layer), _layer_spec(wd, layer)],
        out_specs=[row(di), row(cd), row(LANES)],
        out_shape=[jax.ShapeDtypeStruct((m, di), BF16), jax.ShapeDtypeStruct((m, cd), F32),
                   jax.ShapeDtypeStruct((m, LANES), F32)],
        compiler_params=_params(("parallel",)),
        name="mamba_in",
    )(x, g, w_in, wd)


def _mamba_in_conv_kernel(x_ref, g_ref, w_ref, wd_ref, cw_ref, cb_ref,
                          gate_ref, act_ref, dt_ref, tail_ref, buf, carry, *, tm, L, nc, di):
    t = pl.program_id(1)
    xn = _rms(x_ref[...], g_ref[...]).astype(BF16)
    gate_ref[...] = _silu(_dot(xn, w_ref[:, :di])).astype(BF16)
    dt_ref[...] = _dot(xn, wd_ref[...])
    cw = act_ref.shape[1] // nc

    @pl.when(t == 0)
    def _():
        carry[...] = jnp.zeros_like(carry)

    for c in range(nc):
        cs = slice(c * cw, (c + 1) * cw)
        buf[0:SUBLANES, :] = carry[:, cs]
        buf[SUBLANES:SUBLANES + tm, :] = _dot(xn, w_ref[:, di + c * cw:di + (c + 1) * cw])
        for r in range(tm // L):
            base = SUBLANES + r * L
            conv = cb_ref[:, cs]
            for k in reversed(range(CONV_TAPS)):
                lo = base - (CONV_TAPS - 1) + k
                conv = conv + cw_ref[k:k + 1, cs] * buf[lo:lo + L, :]
            act_ref[r * L:(r + 1) * L, cs] = _silu(conv).astype(BF16)
        last = buf[tm:tm + SUBLANES, :]
        carry[:, cs] = last
        tail_ref[0, :, cs] = last


def _mamba_in_conv(x, layer, g, w_in, wd, cw, cb, di, cd, bsz, tm):
    m, d = x.shape
    nt = m // bsz // tm
    nc = 4
    row = lambda w: pl.BlockSpec((tm, w), lambda b, t: (b * nt + t, 0))
    return pl.pallas_call(
        functools.partial(_mamba_in_conv_kernel, tm=tm, L=min(SSD_CHUNK, tm), nc=nc, di=di),
        grid=(bsz, nt),
        in_specs=[row(d), _layer_spec(g, layer), _layer_spec(w_in, layer), _layer_spec(wd, layer),
                  _layer_spec(cw, layer), _layer_spec(cb, layer)],
        out_specs=[row(di), row(cd), row(LANES), pl.BlockSpec((1, SUBLANES, cd), lambda b, t: (b, 0, 0))],
        out_shape=[jax.ShapeDtypeStruct((m, di), BF16), jax.ShapeDtypeStruct((m, cd), BF16),
                   jax.ShapeDtypeStruct((m, LANES), F32), jax.ShapeDtypeStruct((bsz, SUBLANES, cd), F32)],
        scratch_shapes=[pltpu.VMEM((tm + SUBLANES, cd // nc), F32), pltpu.VMEM((SUBLANES, cd), F32)],
        compiler_params=_params(("parallel", "arbitrary")),
        name="mamba_in_conv",
    )(x, g, w_in, wd, cw, cb)


def _ssd_kernel(act_ref, dt_ref, dtb_ref, alog_ref, dsk_ref, tril_ref, e_ref,
                y_ref, h_ref, *, tt, L, G, EH, P, N):
    t = pl.program_id(1)
    di = G * EH * P
    gw = EH * P

    @pl.when(t == 0)
    def _():
        h_ref[...] = jnp.zeros_like(h_ref)

    a = -jnp.exp(alog_ref[...])
    tril = tril_ref[...]
    emat = e_ref[...]
    rows = lax.broadcasted_iota(jnp.int32, (L, L), 0)
    cols = lax.broadcasted_iota(jnp.int32, (L, L), 1)
    causal = rows >= cols
    lane = lax.broadcasted_iota(jnp.int32, (L, gw), 1)
    head_masks = [(lane // P) == e for e in range(EH)]

    for c in range(tt // L):
        rs = slice(c * L, (c + 1) * L)
        dt = _softplus(dt_ref[rs, :] + dtb_ref[...])
        cum = _dot_split_rhs(tril, dt * a, parts=3)
        cum_t = cum.T
        dt_t = dt.T
        to_end = jnp.exp(cum[L - 1:L, :] - cum)
        expanded = _dot_split_lhs(jnp.concatenate([dt * to_end, jnp.exp(cum)], axis=0), emat)
        xw_t = (act_ref[rs, :di].astype(F32) * expanded[:L]).T.astype(BF16)
        chunk_decay = jnp.broadcast_to(jnp.exp(cum_t[:, L - 1:L]), (LANES, N))

        for g in range(G):
            bg = act_ref[rs, di + g * N:di + (g + 1) * N]
            cg = act_ref[rs, di + (G + g) * N:di + (G + g + 1) * N]
            cb = _dot_nt(cg, bg)
            sl = slice(g * gw, (g + 1) * gw)
            xs_g = act_ref[rs, sl]
            mats, blocks = [], []
            for e in range(EH):
                h = g * EH + e
                seg = cum[:, h:h + 1] - cum_t[h:h + 1, :]
                decay = jnp.exp(jnp.where(causal, seg, -jnp.inf))
                mats.append((cb * decay * dt_t[h:h + 1, :]).astype(BF16))
                blocks.append(jnp.where(head_masks[e], xs_g, jnp.zeros_like(xs_g)))
            y_intra = _dot(jnp.concatenate(mats, axis=1), jnp.concatenate(blocks, axis=0))
            hg = h_ref[0, sl, :]
            y_inter = _dot_nt(cg, hg.astype(BF16)) * expanded[L:, sl]
            y_ref[rs, sl] = (y_intra + y_inter + dsk_ref[:, sl] * xs_g.astype(F32)).astype(y_ref.dtype)
            s_new = _dot(xw_t[sl, :], bg)
            for e in range(EH):
                h = g * EH + e
                hs = slice(e * P, (e + 1) * P)
                h_ref[0, g * gw + e * P:g * gw + (e + 1) * P, :] = hg[hs, :] * chunk_decay[h:h + 1, :] + s_new[hs, :]


def _ssd(act, dt_raw, layer, dtb, alog, dsk, bsz, G, EH, P, N, tt):
    m, cd = act.shape
    nt = m // bsz // tt
    L = min(SSD_CHUNK, tt)
    heads = G * EH
    di = heads * P
    tril = jnp.asarray(np.tile(np.tril(np.ones((L, L), np.float32)), (1, 3)), BF16)
    emat = _head_expand_mat(heads, P, 2)
    row = lambda w: pl.BlockSpec((tt, w), lambda b, t: (b * nt + t, 0))
    return pl.pallas_call(
        functools.partial(_ssd_kernel, tt=tt, L=L, G=G, EH=EH, P=P, N=N),
        grid=(bsz, nt),
        in_specs=[row(cd), row(LANES), _layer_spec(dtb, layer), _layer_spec(alog, layer), _layer_spec(dsk, layer),
                  _const_spec(tril.shape), _const_spec(emat.shape)],
        out_specs=[row(di), pl.BlockSpec((1, di, N), lambda b, t: (b, 0, 0))],
        out_shape=[jax.ShapeDtypeStruct((m, di), BF16), jax.ShapeDtypeStruct((bsz, di, N), F32)],
        compiler_params=_params(("parallel", "arbitrary")),
        name="ssd",
    )(act, dt_raw, dtb, alog, dsk, tril, emat)


def _ssm_step_kernel(*refs, bb, G, EH, P, N, layer, aliased):
    (h0_ref, xbc_ref, c0_ref, dt_ref, cw_ref, cb_ref, dtb_ref, alog_ref, dsk_ref, e_ref) = refs[:10]
    rest = refs[11:] if aliased else refs[10:]
    hn_all, y_ref, cnew_ref, xdt_t_scr, dec_t_scr, b_scr, c_scr, xs_scr = rest
    if aliased:
        hn_ref = hn_all
    else:
        hn_ref = hn_all.at[layer]
        for other in range(hn_all.shape[0]):
            if other != layer:
                hn_all[other] = jnp.zeros(hn_all.shape[1:], F32)
    i = pl.program_id(0)
    cd = xbc_ref.shape[1]
    di = G * EH * P
    gw = EH * P
    db = xbc_ref.shape[0]

    @pl.when(i == 0)
    def _():
        x = xbc_ref[...]
        conv = cb_ref[...] + cw_ref[CONV_TAPS - 1:CONV_TAPS, :] * x
        for k in range(CONV_TAPS - 1):
            conv = conv + cw_ref[k:k + 1, :] * c0_ref[:, k * cd:(k + 1) * cd]
        cnew_ref[:, 0:(CONV_TAPS - 2) * cd] = c0_ref[:, cd:(CONV_TAPS - 1) * cd]
        cnew_ref[:, (CONV_TAPS - 2) * cd:(CONV_TAPS - 1) * cd] = x
        act = _silu(conv)
        xs = act[:, :di]
        xs_scr[...] = xs
        b_scr[...] = act[:, di:di + G * N]
        c_scr[...] = act[:, di + G * N:]
        dt = _softplus(dt_ref[...] + dtb_ref[...])
        decay = jnp.exp(dt * (-jnp.exp(alog_ref[...])))
        xdt_t_scr[...] = (xs * _dot_split_lhs(dt, e_ref[...], parts=3)).T
        dec_t_scr[...] = decay.T

    seq = lax.broadcasted_iota(jnp.int32, (1, db), 1)
    for j in range(bb):
        b = i * bb + j
        pick = seq == b
        xcol = jnp.sum(jnp.where(pick, xdt_t_scr[...], 0.0), axis=1, keepdims=True)
        dcol = jnp.sum(jnp.where(pick, dec_t_scr[...], 0.0), axis=1, keepdims=True)
        brow = b_scr[pl.ds(b, 1), :]
        crow = c_scr[pl.ds(b, 1), :]
        parts = []
        for g in range(G):
            bg = brow[:, g * N:(g + 1) * N]
            pieces = []
            for e in range(EH):
                h = g * EH + e
                rows = slice(h * P, (h + 1) * P)
                pieces.append(h0_ref[j, rows, :] * dcol[h:h + 1, :] + xcol[rows, :] * bg)
            hn = jnp.concatenate(pieces, axis=0)
            hn_ref[j, g * gw:(g + 1) * gw, :] = hn
            c8 = jnp.broadcast_to(crow[:, g * N:(g + 1) * N], (SUBLANES, N)).astype(BF16)
            parts.append(_dot_nt(c8, hn.astype(BF16))[0:1, :])
        y_ref[pl.ds(b, 1), :] = jnp.concatenate(parts, axis=1) + dsk_ref[...] * xs_scr[pl.ds(b, 1), :]


def _ssm_step(h_all, layer, h_prev_out, xbc, c0, dt_raw, cw, cb, dtb, alog, dsk, G, EH, P, N, bb):
    db, cd = xbc.shape
    hp = G * EH * P
    di = hp
    emat = _head_expand_mat(G * EH, P, 3)
    st = pl.BlockSpec((None, bb, hp, N), lambda i: (layer, i, 0, 0))
    st_all = pl.BlockSpec((h_all.shape[0], bb, hp, N), lambda i: (0, i, 0, 0))
    taps = CONV_TAPS - 1
    in_specs = [st, _const_spec((db, cd)), _layer_spec(c0, layer), _const_spec((db, LANES)),
                _layer_spec(cw, layer), _layer_spec(cb, layer), _layer_spec(dtb, layer), _layer_spec(alog, layer),
                _layer_spec(dsk, layer), _const_spec(emat.shape)]
    args = [h_all, xbc, c0, dt_raw, cw, cb, dtb, alog, dsk, emat]
    aliases = {}
    if h_prev_out is not None:
        in_specs.append(pl.BlockSpec(memory_space=pl.ANY))
        args.append(h_prev_out)
        aliases = {len(args) - 1: 0}
    return pl.pallas_call(
        functools.partial(_ssm_step_kernel, bb=bb, G=G, EH=EH, P=P, N=N, layer=layer, aliased=h_prev_out is not None),
        grid=(db // bb,),
        in_specs=in_specs,
        out_specs=[st if h_prev_out is not None else st_all,
                   pl.BlockSpec((db, di), lambda i: (0, 0)), pl.BlockSpec((db, taps * cd), lambda i: (0, 0))],
        out_shape=[jax.ShapeDtypeStruct(h_all.shape, F32), jax.ShapeDtypeStruct((db, di), F32),
                   jax.ShapeDtypeStruct((db, taps * cd), F32)],
        scratch_shapes=[pltpu.VMEM((hp, db), F32), pltpu.VMEM((LANES, db), F32),
                        pltpu.VMEM((db, G * N), F32), pltpu.VMEM((db, G * N), F32), pltpu.VMEM((db, di), F32)],
        input_output_aliases=aliases,
        compiler_params=_params(("arbitrary",)),
        name="ssm_step",
    )(*args)


def _attn_kernel(q_ref, kh_ref, km_ref, vh_ref, vm_ref, sink_ref, o_ref, *, tq, W, KV, GQ, hd):
    t = pl.program_id(1)
    per = LANES // hd
    key = lax.broadcasted_iota(jnp.int32, (W, W), 0)
    r = lax.broadcasted_iota(jnp.int32, (W, W), 1)
    cur_ok = key <= r
    half_k = lax.broadcasted_iota(jnp.int32, (2 * W, LANES), 1) // hd
    half_o = lax.broadcasted_iota(jnp.int32, (LANES, W), 0) // hd
    nblk = tq // W
    for j in range(nblk):
        first = jnp.logical_and(t == 0, j == 0)
        prev_ok = key >= jnp.maximum(r, jnp.where(first, W, 0))
        rows_cur = slice(j * W, (j + 1) * W)
        if j == 0:
            k2 = jnp.concatenate([kh_ref[...], km_ref[rows_cur, :]], axis=0)
            v2 = jnp.concatenate([vh_ref[...], vm_ref[rows_cur, :]], axis=0)
        else:
            k2 = km_ref[(j - 1) * W:(j + 1) * W, :]
            v2 = vm_ref[(j - 1) * W:(j + 1) * W, :]
        for blk in range(KV // per):
            ls = slice(blk * LANES, (blk + 1) * LANES)
            kblk = k2[:, ls]
            k_u = [jnp.where(half_k == u, kblk, 0.0).astype(BF16) for u in range(per)]
            v_t = v2[:, ls].T.astype(BF16)
            for e in range(GQ):
                cb = (e * (KV // per) + blk) * LANES
                qs = q_ref[j * W:(j + 1) * W, cb:cb + LANES]
                acc_t = None
                for u in range(per):
                    slot = e * KV + blk * per + u
                    sink = sink_ref[:, slot:slot + 1]
                    s_t = _dot_nt(k_u[u], qs)
                    s_prev = jnp.where(prev_ok, s_t[:W], -jnp.inf)
                    s_cur = jnp.where(cur_ok, s_t[W:], -jnp.inf)
                    mx = jnp.maximum(jnp.max(jnp.maximum(s_prev, s_cur), axis=0, keepdims=True), sink)
                    p_prev = jnp.exp(s_prev - mx)
                    p_cur = jnp.exp(s_cur - mx)
                    den = jnp.sum(p_prev + p_cur, axis=0, keepdims=True) + jnp.exp(sink - mx)
                    o_t = _dot(v_t, jnp.concatenate([p_prev, p_cur], axis=0).astype(BF16)) / den
                    acc_t = o_t if acc_t is None else jnp.where(half_o == u, o_t, acc_t)
                o_ref[j * W:(j + 1) * W, cb:cb + LANES] = acc_t.T.astype(o_ref.dtype)


def _attn(q, k, v, sinks, layer, bsz, W, KV, GQ, hd, tq):
    m, dq = q.shape
    nt = m // bsz // tq
    dk = KV * hd
    per_tile = tq // W
    row = lambda w: pl.BlockSpec((tq, w), lambda b, t: (b * nt + t, 0))
    head = pl.BlockSpec((W, dk), lambda b, t: (jnp.maximum((b * nt + t) * per_tile - 1, 0), 0))
    return pl.pallas_call(
        functools.partial(_attn_kernel, tq=tq, W=W, KV=KV, GQ=GQ, hd=hd),
        grid=(bsz, nt),
        in_specs=[row(dq), head, row(dk), head, row(dk), _layer_spec(sinks, layer)],
        out_specs=row(dq),
        out_shape=jax.ShapeDtypeStruct((m, dq), BF16),
        compiler_params=_params(("parallel", "parallel")),
        name="attn",
    )(q, k, k, v, v, sinks)


def _attn_step_kernel(q_ref, kc_ref, vc_ref, kn_ref, vn_ref, sink_ref, o_ref, *, bb, KV, GQ, hd):
    nh = KV * GQ
    dq = nh * hd
    dk = KV * hd
    lane_q = lax.broadcasted_iota(jnp.int32, (nh, dq), 1)
    row_q = lax.broadcasted_iota(jnp.int32, (nh, dq), 0)
    own_q = (lane_q // hd) == row_q
    lane_k = lax.broadcasted_iota(jnp.int32, (nh, dk), 1)
    row_k = lax.broadcasted_iota(jnp.int32, (nh, dk), 0)
    own_k = (lane_k // hd) == (row_k % KV)
    sink = sink_ref[...]
    for j in range(bb):
        qm = jnp.where(own_q, jnp.broadcast_to(q_ref[j:j + 1, :], (nh, dq)), 0.0)
        qbd = qm[:, 0:dk]
        for e in range(1, GQ):
            qbd = qbd + qm[:, e * dk:(e + 1) * dk]
        s = _dot_nt(qbd.astype(BF16), kc_ref[j].astype(BF16))
        s_new = jnp.sum(qbd * kn_ref[j:j + 1, :], axis=-1, keepdims=True)
        mx = jnp.maximum(jnp.maximum(jnp.max(s, axis=-1, keepdims=True), s_new), sink)
        p = jnp.exp(s - mx)
        p_new = jnp.exp(s_new - mx)
        den = jnp.sum(p, axis=-1, keepdims=True) + p_new + jnp.exp(sink - mx)
        o = (_dot(p.astype(BF16), vc_ref[j].astype(BF16)) + p_new * vn_ref[j:j + 1, :]) / den
        om = jnp.where(own_k, o, 0.0)
        pieces = [jnp.sum(om[e * KV:(e + 1) * KV, :], axis=0, keepdims=True) for e in range(GQ)]
        o_ref[j:j + 1, :] = jnp.concatenate(pieces, axis=1)


def _attn_step(q, kc, vc, kn, vn, sinks_col, layer, KV, GQ, hd, bb):
    db, dq = q.shape
    W, dk = kc.shape[1], kc.shape[2]
    cache = pl.BlockSpec((bb, W, dk), lambda i: (i, 0, 0))
    rows = lambda w: pl.BlockSpec((bb, w), lambda i: (i, 0))
    return pl.pallas_call(
        functools.partial(_attn_step_kernel, bb=bb, KV=KV, GQ=GQ, hd=hd),
        grid=(db // bb,),
        in_specs=[rows(dq), cache, cache, rows(dk), rows(dk), _layer_spec(sinks_col, layer)],
        out_specs=rows(dq),
        out_shape=jax.ShapeDtypeStruct((db, dq), F32),
        compiler_params=_params(("parallel",)),
        name="attn_step",
    )(q, kc, vc, kn, vn, sinks_col)


def _rope_tables(pos, hd):
    half = hd // 2
    inv = ROPE_THETA ** (-jnp.arange(half, dtype=F32) / half)
    ang = pos.astype(F32)[:, None] * inv[None, :]
    cos, sin = jnp.cos(ang), jnp.sin(ang)
    reps = LANES // hd
    return (jnp.tile(jnp.concatenate([cos, cos], axis=1), (1, reps)),
            jnp.tile(jnp.concatenate([-sin, sin], axis=1), (1, reps)))


def _trunk(x, p, ssm0, conv0, k_hist, v_hist, pos0, w, dims, is_prompt):
    bsz, t_len, d = x.shape
    m = bsz * t_len
    depth, n_a, G, EH, P, N, KV, GQ, hd, W = dims
    di = G * EH * P
    cd = di + 2 * G * N
    dk = KV * hd
    tm = 512 if t_len % 512 == 0 else m
    x = x.reshape(m, d)
    p = p.reshape(depth, m, -1)
    pos = pos0 + (jnp.arange(t_len) if is_prompt else jnp.zeros((m,), jnp.int32))
    cos, sin = _rope_tables(pos, hd)
    ffn1 = (w['ffn1_norm'], w['ffn1_wi'], w['ffn1_wo'])
    new_conv = []
    ssm_out = None
    k_new = v_new = None
    mix_args = None
    for i in range(depth):
        if i < n_a:
            win = (w['mix_norm'], w['ssm_in'], w['ssm_wd'])
            sargs = (w['ssm_dt_bias'], w['ssm_a_log'], w['ssm_dsk'])
            x = _pre(x, i, *ffn1, tm)[0]
            if is_prompt:
                gate, act, dt_raw, tail = _mamba_in_conv(x, i, *win, w['ssm_conv_w'], w['ssm_conv_b'], di, cd, bsz, tm)
                y, h_last = _ssd(act, dt_raw, i, *sargs, bsz, G, EH, P, N, tt=min(512, t_len))
                ssm_out = h_last[None] if ssm_out is None else jnp.concatenate([ssm_out, h_last[None]], axis=0)
                new_conv.append(tail[:, SUBLANES - (CONV_TAPS - 1):])
            else:
                gate, xbc, dt_raw = _mamba_in(x, i, *win, di, cd, tm)
                ssm_out, y, c_last = _ssm_step(ssm0, i, ssm_out, xbc, conv0, dt_raw,
                                               w['ssm_conv_w'], w['ssm_conv_b'], *sargs, G, EH, P, N, bb=4)
                new_conv.append(c_last.reshape(bsz, CONV_TAPS - 1, cd))
            mix_args = (y, gate, w['ssm_norm'], w['ssm_out'], i, G)
        else:
            j = i - n_a
            qargs = (w['mix_norm'], w['w_q'], w['q_norm'], j, cos, sin, hd)
            x, q = _pre(x, i, *ffn1, tm, qargs, q_dtype=BF16 if is_prompt else F32)
            if is_prompt:
                o = _attn(q, k_new, v_new, w['sinks_row'], j, bsz, W, KV, GQ, hd, tq=min(512, t_len))
            else:
                o = _attn_step(q, k_hist, v_hist, k_new, v_new, w['sinks_col'], j, KV, GQ, hd, bb=16)
            mix_args = (o, w['w_o'], j)
        kv_args = (w['kv_norm'], w['w_k'], w['w_v'], w['k_norm'], cos, sin, hd) if i == n_a - 1 else None
        res = _post(x, i, mix_args, (w['ffn2_norm'], w['ffn2_wi'], w['ffn2_wo']),
                    (w['ple_norm'], w['ple_gate'], p, w['ple_proj']), tm, kv_args)
        x = res[0]
        if kv_args is not None:
            k_new, v_new = res[1], res[2]
    if is_prompt:
        k_out = k_new.reshape(bsz, t_len, dk)[:, t_len - W:]
        v_out = v_new.reshape(bsz, t_len, dk)[:, t_len - W:]
    else:
        k_out = jnp.concatenate([k_hist, k_new.reshape(bsz, t_len, dk)], axis=1)[:, -W:]
        v_out = jnp.concatenate([v_hist, v_new.reshape(bsz, t_len, dk)], axis=1)[:, -W:]
    return (x.reshape(bsz, t_len, d), ssm_out.reshape(n_a, bsz, G * EH, P, N), jnp.stack(new_conv),
            k_out.reshape(bsz, W, KV, hd), v_out.reshape(bsz, W, KV, hd))


def kernel(x_prompt, x_sample, state_ssm, state_conv, cache_k, cache_v, p_prompt, p_sample, ffn1_norm, ffn1_wi, ffn1_wo, mix_norm, ffn2_norm, ffn2_wi, ffn2_wo, ple_norm, ple_gate, ple_proj, ssm_in, ssm_conv_w, ssm_conv_b, ssm_dt_bias, ssm_a_log, ssm_d, ssm_norm, ssm_out, kv_norm, w_kv, k_norm, w_q, q_norm, attn_sinks, w_o):
    depth, d = ffn1_norm.shape
    n_a, heads = ssm_a_log.shape
    di = ssm_norm.shape[1]
    cd = ssm_conv_b.shape[1]
    N = state_ssm.shape[-1]
    P = state_ssm.shape[-2]
    G = (cd - di) // (2 * N)
    EH = heads // G
    W, KV, hd = cache_k.shape[1], cache_k.shape[2], cache_k.shape[3]
    nh = w_q.shape[2] // hd
    GQ = nh // KV
    dims = (depth, n_a, G, EH, P, N, KV, GQ, hd, W)

    nb, d_q = w_q.shape[0], w_q.shape[1]
    wq_slot = jnp.swapaxes(w_q.astype(BF16).reshape(nb, d_q, KV, GQ, hd), 2, 3).reshape(nb, d_q, nh * hd)
    wo_slot = jnp.swapaxes(w_o.astype(BF16).reshape(nb, KV, GQ, hd, w_o.shape[2]), 1, 2).reshape(nb, nh * hd, w_o.shape[2])
    sinks_slot = jnp.swapaxes(attn_sinks.reshape(nb, KV, GQ), 1, 2).reshape(nb, nh)
    row1 = lambda v: v[:, None, :]
    pad_heads = lambda v: jnp.pad(v, ((0, 0),) * (v.ndim - 1) + ((0, LANES - heads),))
    dk = KV * hd
    w = {
        'ffn1_norm': row1(ffn1_norm), 'ffn1_wi': ffn1_wi.astype(BF16), 'ffn1_wo': ffn1_wo.astype(BF16),
        'ffn2_norm': row1(ffn2_norm), 'ffn2_wi': ffn2_wi.astype(BF16), 'ffn2_wo': ffn2_wo.astype(BF16),
        'mix_norm': row1(mix_norm), 'ple_norm': row1(ple_norm),
        'ple_gate': ple_gate.astype(BF16), 'ple_proj': ple_proj.astype(BF16),
        'ssm_in': ssm_in.astype(BF16), 'ssm_wd': pad_heads(ssm_in[:, :, di + cd:]).astype(BF16),
        'ssm_conv_w': ssm_conv_w, 'ssm_conv_b': row1(ssm_conv_b),
        'ssm_dt_bias': row1(pad_heads(ssm_dt_bias)), 'ssm_a_log': row1(pad_heads(ssm_a_log)),
        'ssm_dsk': row1(jnp.repeat(ssm_d, P, axis=1)),
        'ssm_norm': row1(ssm_norm), 'ssm_out': ssm_out.astype(BF16),
        'kv_norm': kv_norm[None, :], 'w_k': w_kv[:, :dk].astype(BF16), 'w_v': w_kv[:, dk:].astype(BF16),
        'k_norm': jnp.tile(k_norm, KV)[None, :],
        'w_q': wq_slot, 'q_norm': row1(jnp.tile(q_norm, (1, nh))),
        'sinks_row': row1(jnp.pad(sinks_slot, ((0, 0), (0, LANES - nh)))),
        'sinks_col': sinks_slot[:, :, None],
        'w_o': wo_slot,
    }
    bs = x_sample.shape[0]
    y_p, ssm_p, conv_p, k_p, v_p = _trunk(x_prompt, p_prompt, None, None, None, None, 0, w, dims, True)
    y_s, ssm_s, conv_s, k_s, v_s = _trunk(x_sample, p_sample, state_ssm.reshape(n_a, bs, heads * P, N),
                                          state_conv.reshape(n_a, bs, (CONV_TAPS - 1) * cd),
                                          cache_k.reshape(bs, W, dk), cache_v.reshape(bs, W, dk), PAST_LEN, w, dims, False)
    return (y_p, y_s, ssm_p, conv_p, k_p, v_p, ssm_s, conv_s, k_s, v_s)
```

```python
import functools

import numpy as np
import jax
import jax.numpy as jnp
from jax import lax
from jax.experimental import pallas as pl
from jax.experimental.pallas import tpu as pltpu

F32 = jnp.float32
BF16 = jnp.bfloat16
EPS = 1e-6
PAST_LEN = 8192
ROPE_THETA = 10000.0
LANES = 128
SUBLANES = 8
SSD_CHUNK = 128
CONV_TAPS = 4
VMEM_V7X_BYTES = 64 * 1024 * 1024
VMEM_BYTES = VMEM_V7X_BYTES - 8 * 1024 * 1024

TOKEN_TILE = 512
SPLIT_ROWS = 512
FFN_CHUNK = 256
CONV_SLABS = 4
SSD_TILE = 512
ATTN_TILE = 1024
STATE_SEQS = 4
ATTN_STEP_SEQS = 16


def _dot(a, b):
    return jnp.dot(a, b, preferred_element_type=F32)


def _dot_nt(a, b):
    return lax.dot_general(a, b, (((1,), (1,)), ((), ())), preferred_element_type=F32)


def _split(a, parts):
    pieces = []
    for _ in range(parts):
        piece = a.astype(BF16)
        pieces.append(piece)
        a = a - piece.astype(F32)
    return pieces


def _dot_split_lhs(a, b_stacked, parts=2):
    return _dot(jnp.concatenate(_split(a, parts), axis=1), b_stacked)


def _dot_split_rhs(a_tiled, b, parts=2):
    return _dot(a_tiled, jnp.concatenate(_split(b, parts), axis=0))


def _rms(x, g):
    ms = jnp.mean(x * x, axis=-1, keepdims=True)
    return x * lax.rsqrt(ms + EPS) * g


def _sigmoid(x):
    return 0.5 * jnp.tanh(0.5 * x) + 0.5


def _silu(x):
    half = 0.5 * x
    return half * jnp.tanh(half) + half


def _softplus(v):
    return jnp.maximum(v, 0.0) + jnp.log1p(jnp.exp(-jnp.abs(v)))


def _seg_rms(x, seg_sum, seg_exp, seg):
    ms = _dot((x * x).astype(BF16), seg_sum) * (1.0 / seg)
    return _dot_split_lhs(lax.rsqrt(ms + EPS), seg_exp)


def _rope(x, cos, sin_signed, half):
    width = x.shape[-1]
    lane = lax.broadcasted_iota(jnp.int32, x.shape, 1)
    first = (lane % (2 * half)) < half
    rot = jnp.where(first, pltpu.roll(x, width - half, 1), pltpu.roll(x, half, 1))
    return x * cos + rot * sin_signed


def _const_spec(shape):
    zeros = (0,) * len(shape)
    return pl.BlockSpec(shape, lambda *_: zeros, pipeline_mode=pl.Buffered(1))


def _params(sem):
    return pltpu.CompilerParams(dimension_semantics=sem, vmem_limit_bytes=VMEM_BYTES)


def _seg_mats(width, seg):
    s = np.zeros((width, LANES), np.float32)
    s[np.arange(width), np.arange(width) // seg] = 1.0
    return jnp.asarray(s, BF16), jnp.asarray(np.tile(s.T, (2, 1)), BF16)


def _head_expand_mat(heads, p, parts):
    e = np.zeros((LANES, heads * p), np.float32)
    e[np.arange(heads * p) // p, np.arange(heads * p)] = 1.0
    return jnp.asarray(np.tile(e, (parts, 1)), BF16)


def _half_swiglu(x, g_ref, wi_ref, wo_ref, h_scr, dff, fc, rs=slice(None)):
    xn = _rms(x, g_ref[...]).astype(BF16)
    for c in range(dff // fc):
        gt = _dot(xn, wi_ref[:, c * fc:(c + 1) * fc])
        up = _dot(xn, wi_ref[:, dff + c * fc:dff + (c + 1) * fc])
        h_scr[rs, c * fc:(c + 1) * fc] = (_silu(gt) * up).astype(BF16)
    return x + 0.5 * _dot(h_scr[rs, :], wo_ref[...])


def _normed_rope(v, gain_ref, ssum_ref, sexp_ref, cos_ref, sin_ref, hd, rs):
    vn = v * _seg_rms(v, ssum_ref[...], sexp_ref[...], hd) * gain_ref[...]
    reps = v.shape[1] // LANES
    return _rope(vn, jnp.tile(cos_ref[rs, :], (1, reps)), jnp.tile(sin_ref[rs, :], (1, reps)), hd // 2)


def _layer_spec(arr, layer):
    shape = tuple(arr.shape[1:])
    index = (layer,) + (0,) * len(shape)
    return pl.BlockSpec((None,) + shape, lambda *_: index, pipeline_mode=pl.Buffered(1))


def _pre_kernel(*refs, with_q, dff, fc, hd, scale):
    x_ref, g_ref, wi_ref, wo_ref = refs[:4]
    if with_q:
        mg_ref, wq_ref, qg_ref, ssum_ref, sexp_ref, cos_ref, sin_ref, o_ref, q_ref, h_scr = refs[4:]
    else:
        o_ref, h_scr = refs[4:]
    tm = x_ref.shape[0]
    nsplit = 2 if (with_q and tm % SPLIT_ROWS == 0) else 1
    for sp in range(nsplit):
        rs = slice(sp * (tm // nsplit), (sp + 1) * (tm // nsplit))
        x1 = _half_swiglu(x_ref[rs, :], g_ref, wi_ref, wo_ref, h_scr, dff, fc, rs)
        o_ref[rs, :] = x1
        if with_q:
            q = _dot(_rms(x1, mg_ref[...]).astype(BF16), wq_ref[...])
            qr = _normed_rope(q, qg_ref, ssum_ref, sexp_ref, cos_ref, sin_ref, hd, rs)
            q_ref[rs, :] = (qr * scale).astype(q_ref.dtype)


def _pre(x, layer, g, wi, wo, tm, qargs=None, q_dtype=BF16):
    m, d = x.shape
    dff = wo.shape[1]
    fc = FFN_CHUNK if dff % FFN_CHUNK == 0 else dff
    row = lambda w: pl.BlockSpec((tm, w), lambda i: (i, 0))
    in_specs = [row(d), _layer_spec(g, layer), _layer_spec(wi, layer), _layer_spec(wo, layer)]
    args = [x, g, wi, wo]
    out_specs, out_shape = [row(d)], [jax.ShapeDtypeStruct((m, d), F32)]
    hd = 0
    if qargs is not None:
        mg, wq, qg, qlayer, cos, sin, hd = qargs
        dq = wq.shape[2]
        ssum, sexp = _seg_mats(dq, hd)
        nt = cos.shape[0] // tm
        tab = pl.BlockSpec((tm, LANES), lambda i: (i % nt, 0))
        in_specs += [_layer_spec(mg, layer), _layer_spec(wq, qlayer), _layer_spec(qg, qlayer),
                     _const_spec(ssum.shape), _const_spec(sexp.shape), tab, tab]
        args += [mg, wq, qg, ssum, sexp, cos, sin]
        out_specs.append(row(dq))
        out_shape.append(jax.ShapeDtypeStruct((m, dq), q_dtype))
    return pl.pallas_call(
        functools.partial(_pre_kernel, with_q=qargs is not None, dff=dff, fc=fc, hd=hd, scale=hd ** -0.5 if hd else 1.0),
        grid=(m // tm,),
        in_specs=in_specs, out_specs=out_specs, out_shape=out_shape,
        scratch_shapes=[pltpu.VMEM((tm, dff), BF16)],
        compiler_params=_params(("parallel",)),
        name="pre",
    )(*args)


def _post_kernel(*refs, mamba, with_kv, dff, fc, gw, hd):
    it = iter(refs)
    x_ref = next(it)
    if mamba:
        y_ref, gate_ref, ng_ref, ssum_ref, sexp_ref, wmix_ref = [next(it) for _ in range(6)]
    else:
        att_ref, wmix_ref = next(it), next(it)
    g2_ref, wi_ref, wo_ref, pg_ref, wg_ref, p_ref, wp_ref = [next(it) for _ in range(7)]
    if with_kv:
        kvg_ref, wk_ref, wv_ref, kg_ref, ksum_ref, kexp_ref, cos_ref, sin_ref = [next(it) for _ in range(8)]
    o_ref = next(it)
    if with_kv:
        k_ref, v_ref = next(it), next(it)
    h_scr = next(it)

    tm = x_ref.shape[0]
    nsplit = 2 if (mamba and tm % SPLIT_ROWS == 0) else 1
    for sp in range(nsplit):
        rs = slice(sp * (tm // nsplit), (sp + 1) * (tm // nsplit))
        if mamba:
            gated = y_ref[rs, :].astype(F32) * gate_ref[rs, :].astype(F32)
            r = _seg_rms(gated, ssum_ref[...], sexp_ref[...], gw)
            mix = _dot((gated * r * ng_ref[...]).astype(BF16), wmix_ref[...])
        else:
            mix = _dot(att_ref[rs, :].astype(BF16), wmix_ref[...])
        x = _half_swiglu(x_ref[rs, :] + mix, g2_ref, wi_ref, wo_ref, h_scr, dff, fc, rs)
        gate = _sigmoid(_dot(_rms(x, pg_ref[...]).astype(BF16), wg_ref[...]))
        x = x + gate * _dot(p_ref[rs, :].astype(BF16), wp_ref[...])
        o_ref[rs, :] = x
        if with_kv:
            xn = _rms(x, kvg_ref[...]).astype(BF16)
            v_ref[rs, :] = _dot(xn, wv_ref[...])
            k_ref[rs, :] = _normed_rope(_dot(xn, wk_ref[...]), kg_ref, ksum_ref, kexp_ref, cos_ref, sin_ref, hd, rs)


def _post(x, layer, mix_args, ffn_args, ple_args, tm, kv_args=None):
    m, d = x.shape
    g2, wi, wo = ffn_args
    pg, wg, p, wp = ple_args
    dff = wo.shape[1]
    fc = FFN_CHUNK if dff % FFN_CHUNK == 0 else dff
    dp = p.shape[2]
    row = lambda w: pl.BlockSpec((tm, w), lambda i: (i, 0))
    in_specs, args = [row(d)], [x]
    mamba = len(mix_args) == 6
    gw = hd = 0
    if mamba:
        y, gate, ng, wout, mlayer, groups = mix_args
        di = y.shape[1]
        gw = di // groups
        ssum, sexp = _seg_mats(di, gw)
        in_specs += [row(di), row(di), _layer_spec(ng, mlayer), _const_spec(ssum.shape), _const_spec(sexp.shape),
                     _layer_spec(wout, mlayer)]
        args += [y, gate, ng, ssum, sexp, wout]
    else:
        att, wmix, mlayer = mix_args
        in_specs += [row(att.shape[1]), _layer_spec(wmix, mlayer)]
        args += [att, wmix]
    in_specs += [_layer_spec(g2, layer), _layer_spec(wi, layer), _layer_spec(wo, layer),
                 _layer_spec(pg, layer), _layer_spec(wg, layer),
                 pl.BlockSpec((None, tm, dp), lambda i: (layer, i, 0)), _layer_spec(wp, layer)]
    args += [g2, wi, wo, pg, wg, p, wp]
    out_specs, out_shape = [row(d)], [jax.ShapeDtypeStruct((m, d), F32)]
    if kv_args is not None:
        kvg, wk, wv, kg, cos, sin, hd = kv_args
        dk = wk.shape[1]
        ksum, kexp = _seg_mats(dk, hd)
        nt = cos.shape[0] // tm
        tab = pl.BlockSpec((tm, LANES), lambda i: (i % nt, 0))
        in_specs += [_const_spec((1, d)), _const_spec((d, dk)), _const_spec((d, dk)), _const_spec((1, dk)),
                     _const_spec(ksum.shape), _const_spec(kexp.shape), tab, tab]
        args += [kvg, wk, wv, kg, ksum, kexp, cos, sin]
        out_specs += [row(dk), row(dk)]
        out_shape += [jax.ShapeDtypeStruct((m, dk), F32)] * 2
    return pl.pallas_call(
        functools.partial(_post_kernel, mamba=mamba, with_kv=kv_args is not None, dff=dff, fc=fc, gw=gw, hd=hd),
        grid=(m // tm,),
        in_specs=in_specs, out_specs=out_specs, out_shape=out_shape,
        scratch_shapes=[pltpu.VMEM((tm, dff), BF16)],
        compiler_params=_params(("parallel",)),
        name="post",
    )(*args)


def _mamba_in_kernel(x_ref, g_ref, w_ref, wd_ref, gate_ref, xbc_ref, dt_ref, *, nc, di):
    xn = _rms(x_ref[...], g_ref[...]).astype(BF16)
    gate_ref[...] = _silu(_dot(xn, w_ref[:, :di])).astype(BF16)
    cw = xbc_ref.shape[1] // nc
    for c in range(nc):
        xbc_ref[:, c * cw:(c + 1) * cw] = _dot(xn, w_ref[:, di + c * cw:di + (c + 1) * cw])
    dt_ref[...] = _dot(xn, wd_ref[...])


def _mamba_in(x, layer, g, w_in, wd, di, cd, tm):
    m, d = x.shape
    row = lambda w: pl.BlockSpec((tm, w), lambda i: (i, 0))
    return pl.pallas_call(
        functools.partial(_mamba_in_kernel, nc=CONV_SLABS, di=di),
        grid=(m // tm,),
        in_specs=[row(d), _layer_spec(g, layer), _layer_spec(w_in, layer), _layer_spec(wd, layer)],
        out_specs=[row(di), row(cd), row(LANES)],
        out_shape=[jax.ShapeDtypeStruct((m, di), BF16), jax.ShapeDtypeStruct((m, cd), F32),
                   jax.ShapeDtypeStruct((m, LANES), F32)],
        compiler_params=_params(("parallel",)),
        name="mamba_in",
    )(x, g, w_in, wd)


def _mamba_in_conv_kernel(x_ref, g_ref, w_ref, wd_ref, cw_ref, cb_ref,
                          gate_ref, act_ref, dt_ref, tail_ref, buf, carry, *, tm, L, nc, di):
    t = pl.program_id(1)
    xn = _rms(x_ref[...], g_ref[...]).astype(BF16)
    gate_ref[...] = _silu(_dot(xn, w_ref[:, :di])).astype(BF16)
    dt_ref[...] = _dot(xn, wd_ref[...])
    cw = act_ref.shape[1] // nc

    @pl.when(t == 0)
    def _():
        carry[...] = jnp.zeros_like(carry)

    for c in range(nc):
        cs = slice(c * cw, (c + 1) * cw)
        buf[0:SUBLANES, :] = carry[:, cs]
        buf[SUBLANES:SUBLANES + tm, :] = _dot(xn, w_ref[:, di + c * cw:di + (c + 1) * cw])
        for r in range(tm // L):
            base = SUBLANES + r * L
            conv = cb_ref[:, cs]
            for k in reversed(range(CONV_TAPS)):
                lo = base - (CONV_TAPS - 1) + k
                conv = conv + cw_ref[k:k + 1, cs] * buf[lo:lo + L, :]
            act_ref[r * L:(r + 1) * L, cs] = _silu(conv).astype(BF16)
        last = buf[tm:tm + SUBLANES, :]
        carry[:, cs] = last
        tail_ref[0, :, cs] = last


def _mamba_in_conv(x, layer, g, w_in, wd, cw, cb, di, cd, bsz, tm):
    m, d = x.shape
    nt = m // bsz // tm
    nc = CONV_SLABS
    row = lambda w: pl.BlockSpec((tm, w), lambda b, t: (b * nt + t, 0))
    return pl.pallas_call(
        functools.partial(_mamba_in_conv_kernel, tm=tm, L=min(SSD_CHUNK, tm), nc=nc, di=di),
        grid=(bsz, nt),
        in_specs=[row(d), _layer_spec(g, layer), _layer_spec(w_in, layer), _layer_spec(wd, layer),
                  _layer_spec(cw, layer), _layer_spec(cb, layer)],
        out_specs=[row(di), row(cd), row(LANES), pl.BlockSpec((1, SUBLANES, cd), lambda b, t: (b, 0, 0))],
        out_shape=[jax.ShapeDtypeStruct((m, di), BF16), jax.ShapeDtypeStruct((m, cd), BF16),
                   jax.ShapeDtypeStruct((m, LANES), F32), jax.ShapeDtypeStruct((bsz, SUBLANES, cd), F32)],
        scratch_shapes=[pltpu.VMEM((tm + SUBLANES, cd // nc), F32), pltpu.VMEM((SUBLANES, cd), F32)],
        compiler_params=_params(("parallel", "arbitrary")),
        name="mamba_in_conv",
    )(x, g, w_in, wd, cw, cb)


def _ssd_kernel(act_ref, dt_ref, dtb_ref, alog_ref, dsk_ref, tril_ref, e_ref,
                y_ref, h_ref, *, tt, L, G, EH, P, N):
    t = pl.program_id(1)
    di = G * EH * P
    gw = EH * P

    @pl.when(t == 0)
    def _():
        h_ref[...] = jnp.zeros_like(h_ref)

    a = -jnp.exp(alog_ref[...])
    tril = tril_ref[...]
    emat = e_ref[...]
    rows = lax.broadcasted_iota(jnp.int32, (L, L), 0)
    cols = lax.broadcasted_iota(jnp.int32, (L, L), 1)
    causal = rows >= cols
    lane = lax.broadcasted_iota(jnp.int32, (L, gw), 1)
    head_masks = [(lane // P) == e for e in range(EH)]

    for c in range(tt // L):
        rs = slice(c * L, (c + 1) * L)
        dt = _softplus(dt_ref[rs, :] + dtb_ref[...])
        cum = _dot_split_rhs(tril, dt * a, parts=3)
        cum_t = cum.T
        dt_t = dt.T
        to_end = jnp.exp(cum[L - 1:L, :] - cum)
        expanded = _dot_split_lhs(jnp.concatenate([dt * to_end, jnp.exp(cum)], axis=0), emat)
        xw_t = (act_ref[rs, :di].astype(F32) * expanded[:L]).T.astype(BF16)
        chunk_decay = jnp.broadcast_to(jnp.exp(cum_t[:, L - 1:L]), (LANES, N))

        for g in range(G):
            bg = act_ref[rs, di + g * N:di + (g + 1) * N]
            cg = act_ref[rs, di + (G + g) * N:di + (G + g + 1) * N]
            cb = _dot_nt(cg, bg)
            sl = slice(g * gw, (g + 1) * gw)
            xs_g = act_ref[rs, sl]
            mats, blocks = [], []
            for e in range(EH):
                h = g * EH + e
                seg = cum[:, h:h + 1] - cum_t[h:h + 1, :]
                decay = jnp.exp(jnp.where(causal, seg, -jnp.inf))
                mats.append((cb * decay * dt_t[h:h + 1, :]).astype(BF16))
                blocks.append(jnp.where(head_masks[e], xs_g, jnp.zeros_like(xs_g)))
            y_intra = _dot(jnp.concatenate(mats, axis=1), jnp.concatenate(blocks, axis=0))
            hg = h_ref[0, sl, :]
            y_inter = _dot_nt(cg, hg.astype(BF16)) * expanded[L:, sl]
            y_ref[rs, sl] = (y_intra + y_inter + dsk_ref[:, sl] * xs_g.astype(F32)).astype(y_ref.dtype)
            s_new = _dot(xw_t[sl, :], bg)
            for e in range(EH):
                h = g * EH + e
                hs = slice(e * P, (e + 1) * P)
                h_ref[0, g * gw + e * P:g * gw + (e + 1) * P, :] = hg[hs, :] * chunk_decay[h:h + 1, :] + s_new[hs, :]


def _ssd(act, dt_raw, layer, dtb, alog, dsk, bsz, G, EH, P, N, tt):
    m, cd = act.shape
    nt = m // bsz // tt
    L = min(SSD_CHUNK, tt)
    heads = G * EH
    di = heads * P
    tril = jnp.asarray(np.tile(np.tril(np.ones((L, L), np.float32)), (1, 3)), BF16)
    emat = _head_expand_mat(heads, P, 2)
    row = lambda w: pl.BlockSpec((tt, w), lambda b, t: (b * nt + t, 0))
    return pl.pallas_call(
        functools.partial(_ssd_kernel, tt=tt, L=L, G=G, EH=EH, P=P, N=N),
        grid=(bsz, nt),
        in_specs=[row(cd), row(LANES), _layer_spec(dtb, layer), _layer_spec(alog, layer), _layer_spec(dsk, layer),
                  _const_spec(tril.shape), _const_spec(emat.shape)],
        out_specs=[row(di), pl.BlockSpec((1, di, N), lambda b, t: (b, 0, 0))],
        out_shape=[jax.ShapeDtypeStruct((m, di), BF16), jax.ShapeDtypeStruct((bsz, di, N), F32)],
        compiler_params=_params(("parallel", "arbitrary")),
        name="ssd",
    )(act, dt_raw, dtb, alog, dsk, tril, emat)


def _ssm_step_kernel(*refs, bb, G, EH, P, N, layer, aliased):
    (h0_ref, xbc_ref, c0_ref, dt_ref, cw_ref, cb_ref, dtb_ref, alog_ref, dsk_ref, e_ref) = refs[:10]
    rest = refs[11:] if aliased else refs[10:]
    hn_all, y_ref, cnew_ref, xdt_t_scr, dec_t_scr, b_scr, c_scr, xs_scr = rest
    if aliased:
        hn_ref = hn_all
    else:
        hn_ref = hn_all.at[layer]
        for other in range(hn_all.shape[0]):
            if other != layer:
                hn_all[other] = jnp.zeros(hn_all.shape[1:], F32)
    i = pl.program_id(0)
    cd = xbc_ref.shape[1]
    di = G * EH * P
    gw = EH * P
    db = xbc_ref.shape[0]

    @pl.when(i == 0)
    def _():
        x = xbc_ref[...]
        conv = cb_ref[...] + cw_ref[CONV_TAPS - 1:CONV_TAPS, :] * x
        for k in range(CONV_TAPS - 1):
            conv = conv + cw_ref[k:k + 1, :] * c0_ref[:, k * cd:(k + 1) * cd]
        cnew_ref[:, 0:(CONV_TAPS - 2) * cd] = c0_ref[:, cd:(CONV_TAPS - 1) * cd]
        cnew_ref[:, (CONV_TAPS - 2) * cd:(CONV_TAPS - 1) * cd] = x
        act = _silu(conv)
        xs = act[:, :di]
        xs_scr[...] = xs
        b_scr[...] = act[:, di:di + G * N]
        c_scr[...] = act[:, di + G * N:]
        dt = _softplus(dt_ref[...] + dtb_ref[...])
        decay = jnp.exp(dt * (-jnp.exp(alog_ref[...])))
        xdt_t_scr[...] = (xs * _dot_split_lhs(dt, e_ref[...], parts=3)).T
        dec_t_scr[...] = decay.T

    seq = lax.broadcasted_iota(jnp.int32, (1, db), 1)
    for j in range(bb):
        b = i * bb + j
        pick = seq == b
        xcol = jnp.sum(jnp.where(pick, xdt_t_scr[...], 0.0), axis=1, keepdims=True)
        dcol = jnp.sum(jnp.where(pick, dec_t_scr[...], 0.0), axis=1, keepdims=True)
        brow = b_scr[pl.ds(b, 1), :]
        crow = c_scr[pl.ds(b, 1), :]
        parts = []
        for g in range(G):
            bg = brow[:, g * N:(g + 1) * N]
            pieces = []
            for e in range(EH):
                h = g * EH + e
                rows = slice(h * P, (h + 1) * P)
                pieces.append(h0_ref[j, rows, :] * dcol[h:h + 1, :] + xcol[rows, :] * bg)
            hn = jnp.concatenate(pieces, axis=0)
            hn_ref[j, g * gw:(g + 1) * gw, :] = hn
            c8 = jnp.broadcast_to(crow[:, g * N:(g + 1) * N], (SUBLANES, N)).astype(BF16)
            parts.append(_dot_nt(c8, hn.astype(BF16))[0:1, :])
        y_ref[pl.ds(b, 1), :] = jnp.concatenate(parts, axis=1) + dsk_ref[...] * xs_scr[pl.ds(b, 1), :]


def _ssm_step(h_all, layer, h_prev_out, xbc, c0, dt_raw, cw, cb, dtb, alog, dsk, G, EH, P, N, bb):
    db, cd = xbc.shape
    hp = G * EH * P
    di = hp
    emat = _head_expand_mat(G * EH, P, 3)
    st = pl.BlockSpec((None, bb, hp, N), lambda i: (layer, i, 0, 0))
    st_all = pl.BlockSpec((h_all.shape[0], bb, hp, N), lambda i: (0, i, 0, 0))
    taps = CONV_TAPS - 1
    in_specs = [st, _const_spec((db, cd)), _layer_spec(c0, layer), _const_spec((db, LANES)),
                _layer_spec(cw, layer), _layer_spec(cb, layer), _layer_spec(dtb, layer), _layer_spec(alog, layer),
                _layer_spec(dsk, layer), _const_spec(emat.shape)]
    args = [h_all, xbc, c0, dt_raw, cw, cb, dtb, alog, dsk, emat]
    aliases = {}
    if h_prev_out is not None:
        in_specs.append(pl.BlockSpec(memory_space=pl.ANY))
        args.append(h_prev_out)
        aliases = {len(args) - 1: 0}
    return pl.pallas_call(
        functools.partial(_ssm_step_kernel, bb=bb, G=G, EH=EH, P=P, N=N, layer=layer, aliased=h_prev_out is not None),
        grid=(db // bb,),
        in_specs=in_specs,
        out_specs=[st if h_prev_out is not None else st_all,
                   pl.BlockSpec((db, di), lambda i: (0, 0)), pl.BlockSpec((db, taps * cd), lambda i: (0, 0))],
        out_shape=[jax.ShapeDtypeStruct(h_all.shape, F32), jax.ShapeDtypeStruct((db, di), F32),
                   jax.ShapeDtypeStruct((db, taps * cd), F32)],
        scratch_shapes=[pltpu.VMEM((hp, db), F32), pltpu.VMEM((LANES, db), F32),
                        pltpu.VMEM((db, G * N), F32), pltpu.VMEM((db, G * N), F32), pltpu.VMEM((db, di), F32)],
        input_output_aliases=aliases,
        compiler_params=_params(("arbitrary",)),
        name="ssm_step",
    )(*args)


def _attn_kernel(q_ref, kh_ref, km_ref, vh_ref, vm_ref, sink_ref, o_ref, *, tq, W, KV, GQ, hd):
    t = pl.program_id(1)
    per = LANES // hd
    key = lax.broadcasted_iota(jnp.int32, (W, W), 0)
    r = lax.broadcasted_iota(jnp.int32, (W, W), 1)
    cur_ok = key <= r
    half_k = lax.broadcasted_iota(jnp.int32, (2 * W, LANES), 1) // hd
    half_o = lax.broadcasted_iota(jnp.int32, (LANES, W), 0) // hd
    nblk = tq // W
    for j in range(nblk):
        first = jnp.logical_and(t == 0, j == 0)
        prev_ok = key >= jnp.maximum(r, jnp.where(first, W, 0))
        rows_cur = slice(j * W, (j + 1) * W)
        if j == 0:
            k2 = jnp.concatenate([kh_ref[...], km_ref[rows_cur, :]], axis=0)
            v2 = jnp.concatenate([vh_ref[...], vm_ref[rows_cur, :]], axis=0)
        else:
            k2 = km_ref[(j - 1) * W:(j + 1) * W, :]
            v2 = vm_ref[(j - 1) * W:(j + 1) * W, :]
        for blk in range(KV // per):
            ls = slice(blk * LANES, (blk + 1) * LANES)
            kblk = k2[:, ls]
            k_u = [jnp.where(half_k == u, kblk, 0.0).astype(BF16) for u in range(per)]
            v_t = v2[:, ls].T.astype(BF16)
            for e in range(GQ):
                cb = (e * (KV // per) + blk) * LANES
                qs = q_ref[j * W:(j + 1) * W, cb:cb + LANES]
                acc_t = None
                for u in range(per):
                    slot = e * KV + blk * per + u
                    sink = sink_ref[:, slot:slot + 1]
                    s_t = _dot_nt(k_u[u], qs)
                    s_prev = jnp.where(prev_ok, s_t[:W], -jnp.inf)
                    s_cur = jnp.where(cur_ok, s_t[W:], -jnp.inf)
                    mx = jnp.maximum(jnp.max(jnp.maximum(s_prev, s_cur), axis=0, keepdims=True), sink)
                    p_prev = jnp.exp(s_prev - mx)
                    p_cur = jnp.exp(s_cur - mx)
                    den = jnp.sum(p_prev + p_cur, axis=0, keepdims=True) + jnp.exp(sink - mx)
                    o_t = _dot(v_t, jnp.concatenate([p_prev, p_cur], axis=0).astype(BF16)) / den
                    acc_t = o_t if acc_t is None else jnp.where(half_o == u, o_t, acc_t)
                o_ref[j * W:(j + 1) * W, cb:cb + LANES] = acc_t.T.astype(o_ref.dtype)


def _attn(q, k, v, sinks, layer, bsz, W, KV, GQ, hd, tq):
    m, dq = q.shape
    nt = m // bsz // tq
    dk = KV * hd
    per_tile = tq // W
    row = lambda w: pl.BlockSpec((tq, w), lambda b, t: (b * nt + t, 0))
    head = pl.BlockSpec((W, dk), lambda b, t: (jnp.maximum((b * nt + t) * per_tile - 1, 0), 0))
    return pl.pallas_call(
        functools.partial(_attn_kernel, tq=tq, W=W, KV=KV, GQ=GQ, hd=hd),
        grid=(bsz, nt),
        in_specs=[row(dq), head, row(dk), head, row(dk), _layer_spec(sinks, layer)],
        out_specs=row(dq),
        out_shape=jax.ShapeDtypeStruct((m, dq), BF16),
        compiler_params=_params(("parallel", "parallel")),
        name="attn",
    )(q, k, k, v, v, sinks)


def _attn_step_kernel(q_ref, kc_ref, vc_ref, kn_ref, vn_ref, sink_ref, o_ref, *, bb, KV, GQ, hd):
    nh = KV * GQ
    dq = nh * hd
    dk = KV * hd
    lane_q = lax.broadcasted_iota(jnp.int32, (nh, dq), 1)
    row_q = lax.broadcasted_iota(jnp.int32, (nh, dq), 0)
    own_q = (lane_q // hd) == row_q
    lane_k = lax.broadcasted_iota(jnp.int32, (nh, dk), 1)
    row_k = lax.broadcasted_iota(jnp.int32, (nh, dk), 0)
    own_k = (lane_k // hd) == (row_k % KV)
    sink = sink_ref[...]
    for j in range(bb):
        qm = jnp.where(own_q, jnp.broadcast_to(q_ref[j:j + 1, :], (nh, dq)), 0.0)
        qbd = qm[:, 0:dk]
        for e in range(1, GQ):
            qbd = qbd + qm[:, e * dk:(e + 1) * dk]
        s = _dot_nt(qbd.astype(BF16), kc_ref[j].astype(BF16))
        s_new = jnp.sum(qbd * kn_ref[j:j + 1, :], axis=-1, keepdims=True)
        mx = jnp.maximum(jnp.maximum(jnp.max(s, axis=-1, keepdims=True), s_new), sink)
        p = jnp.exp(s - mx)
        p_new = jnp.exp(s_new - mx)
        den = jnp.sum(p, axis=-1, keepdims=True) + p_new + jnp.exp(sink - mx)
        o = (_dot(p.astype(BF16), vc_ref[j].astype(BF16)) + p_new * vn_ref[j:j + 1, :]) / den
        om = jnp.where(own_k, o, 0.0)
        pieces = [jnp.sum(om[e * KV:(e + 1) * KV, :], axis=0, keepdims=True) for e in range(GQ)]
        o_ref[j:j + 1, :] = jnp.concatenate(pieces, axis=1)


def _attn_step(q, kc, vc, kn, vn, sinks_col, layer, KV, GQ, hd, bb):
    db, dq = q.shape
    W, dk = kc.shape[1], kc.shape[2]
    cache = pl.BlockSpec((bb, W, dk), lambda i: (i, 0, 0))
    rows = lambda w: pl.BlockSpec((bb, w), lambda i: (i, 0))
    return pl.pallas_call(
        functools.partial(_attn_step_kernel, bb=bb, KV=KV, GQ=GQ, hd=hd),
        grid=(db // bb,),
        in_specs=[rows(dq), cache, cache, rows(dk), rows(dk), _layer_spec(sinks_col, layer)],
        out_specs=rows(dq),
        out_shape=jax.ShapeDtypeStruct((db, dq), F32),
        compiler_params=_params(("parallel",)),
        name="attn_step",
    )(q, kc, vc, kn, vn, sinks_col)


def _rope_tables(pos, hd):
    half = hd // 2
    inv = ROPE_THETA ** (-jnp.arange(half, dtype=F32) / half)
    ang = pos.astype(F32)[:, None] * inv[None, :]
    cos, sin = jnp.cos(ang), jnp.sin(ang)
    reps = LANES // hd
    return (jnp.tile(jnp.concatenate([cos, cos], axis=1), (1, reps)),
            jnp.tile(jnp.concatenate([-sin, sin], axis=1), (1, reps)))


def _trunk(x, p, ssm0, conv0, k_hist, v_hist, pos0, w, dims, is_prompt):
    bsz, t_len, d = x.shape
    m = bsz * t_len
    depth, n_a, G, EH, P, N, KV, GQ, hd, W = dims
    di = G * EH * P
    cd = di + 2 * G * N
    dk = KV * hd
    tm = TOKEN_TILE if t_len % TOKEN_TILE == 0 else m
    x = x.reshape(m, d)
    p = p.reshape(depth, m, -1)
    pos = pos0 + (jnp.arange(t_len) if is_prompt else jnp.zeros((m,), jnp.int32))
    cos, sin = _rope_tables(pos, hd)
    ffn1 = (w['ffn1_norm'], w['ffn1_wi'], w['ffn1_wo'])
    new_conv = []
    ssm_out = None
    k_new = v_new = None
    mix_args = None
    for i in range(depth):
        if i < n_a:
            win = (w['mix_norm'], w['ssm_in'], w['ssm_wd'])
            sargs = (w['ssm_dt_bias'], w['ssm_a_log'], w['ssm_dsk'])
            x = _pre(x, i, *ffn1, tm)[0]
            if is_prompt:
                gate, act, dt_raw, tail = _mamba_in_conv(x, i, *win, w['ssm_conv_w'], w['ssm_conv_b'], di, cd, bsz, tm)
                y, h_last = _ssd(act, dt_raw, i, *sargs, bsz, G, EH, P, N, tt=min(SSD_TILE, t_len))
                ssm_out = h_last[None] if ssm_out is None else jnp.concatenate([ssm_out, h_last[None]], axis=0)
                new_conv.append(tail[:, SUBLANES - (CONV_TAPS - 1):])
            else:
                gate, xbc, dt_raw = _mamba_in(x, i, *win, di, cd, tm)
                ssm_out, y, c_last = _ssm_step(ssm0, i, ssm_out, xbc, conv0, dt_raw,
                                               w['ssm_conv_w'], w['ssm_conv_b'], *sargs, G, EH, P, N, bb=STATE_SEQS)
                new_conv.append(c_last.reshape(bsz, CONV_TAPS - 1, cd))
            mix_args = (y, gate, w['ssm_norm'], w['ssm_out'], i, G)
        else:
            j = i - n_a
            qargs = (w['mix_norm'], w['w_q'], w['q_norm'], j, cos, sin, hd)
            x, q = _pre(x, i, *ffn1, tm, qargs, q_dtype=BF16 if is_prompt else F32)
            if is_prompt:
                o = _attn(q, k_new, v_new, w['sinks_row'], j, bsz, W, KV, GQ, hd, tq=min(ATTN_TILE, t_len))
            else:
                o = _attn_step(q, k_hist, v_hist, k_new, v_new, w['sinks_col'], j, KV, GQ, hd, bb=ATTN_STEP_SEQS)
            mix_args = (o, w['w_o'], j)
        kv_args = (w['kv_norm'], w['w_k'], w['w_v'], w['k_norm'], cos, sin, hd) if i == n_a - 1 else None
        res = _post(x, i, mix_args, (w['ffn2_norm'], w['ffn2_wi'], w['ffn2_wo']),
                    (w['ple_norm'], w['ple_gate'], p, w['ple_proj']), tm, kv_args)
        x = res[0]
        if kv_args is not None:
            k_new, v_new = res[1], res[2]
    if is_prompt:
        k_out = k_new.reshape(bsz, t_len, dk)[:, t_len - W:]
        v_out = v_new.reshape(bsz, t_len, dk)[:, t_len - W:]
    else:
        k_out = jnp.concatenate([k_hist, k_new.reshape(bsz, t_len, dk)], axis=1)[:, -W:]
        v_out = jnp.concatenate([v_hist, v_new.reshape(bsz, t_len, dk)], axis=1)[:, -W:]
    return (x.reshape(bsz, t_len, d), ssm_out.reshape(n_a, bsz, G * EH, P, N), jnp.stack(new_conv),
            k_out.reshape(bsz, W, KV, hd), v_out.reshape(bsz, W, KV, hd))


def kernel(x_prompt, x_sample, state_ssm, state_conv, cache_k, cache_v, p_prompt, p_sample, ffn1_norm, ffn1_wi, ffn1_wo, mix_norm, ffn2_norm, ffn2_wi, ffn2_wo, ple_norm, ple_gate, ple_proj, ssm_in, ssm_conv_w, ssm_conv_b, ssm_dt_bias, ssm_a_log, ssm_d, ssm_norm, ssm_out, kv_norm, w_kv, k_norm, w_q, q_norm, attn_sinks, w_o):
    depth, d = ffn1_norm.shape
    n_a, heads = ssm_a_log.shape
    di = ssm_norm.shape[1]
    cd = ssm_conv_b.shape[1]
    N = state_ssm.shape[-1]
    P = state_ssm.shape[-2]
    G = (cd - di) // (2 * N)
    EH = heads // G
    W, KV, hd = cache_k.shape[1], cache_k.shape[2], cache_k.shape[3]
    nh = w_q.shape[2] // hd
    GQ = nh // KV
    dims = (depth, n_a, G, EH, P, N, KV, GQ, hd, W)

    nb, d_q = w_q.shape[0], w_q.shape[1]
    wq_slot = jnp.swapaxes(w_q.astype(BF16).reshape(nb, d_q, KV, GQ, hd), 2, 3).reshape(nb, d_q, nh * hd)
    wo_slot = jnp.swapaxes(w_o.astype(BF16).reshape(nb, KV, GQ, hd, w_o.shape[2]), 1, 2).reshape(nb, nh * hd, w_o.shape[2])
    sinks_slot = jnp.swapaxes(attn_sinks.reshape(nb, KV, GQ), 1, 2).reshape(nb, nh)
    row1 = lambda v: v[:, None, :]
    pad_heads = lambda v: jnp.pad(v, ((0, 0),) * (v.ndim - 1) + ((0, LANES - heads),))
    dk = KV * hd
    w = {
        'ffn1_norm': row1(ffn1_norm), 'ffn1_wi': ffn1_wi.astype(BF16), 'ffn1_wo': ffn1_wo.astype(BF16),
        'ffn2_norm': row1(ffn2_norm), 'ffn2_wi': ffn2_wi.astype(BF16), 'ffn2_wo': ffn2_wo.astype(BF16),
        'mix_norm': row1(mix_norm), 'ple_norm': row1(ple_norm),
        'ple_gate': ple_gate.astype(BF16), 'ple_proj': ple_proj.astype(BF16),
        'ssm_in': ssm_in.astype(BF16), 'ssm_wd': pad_heads(ssm_in[:, :, di + cd:]).astype(BF16),
        'ssm_conv_w': ssm_conv_w, 'ssm_conv_b': row1(ssm_conv_b),
        'ssm_dt_bias': row1(pad_heads(ssm_dt_bias)), 'ssm_a_log': row1(pad_heads(ssm_a_log)),
        'ssm_dsk': row1(jnp.repeat(ssm_d, P, axis=1)),
        'ssm_norm': row1(ssm_norm), 'ssm_out': ssm_out.astype(BF16),
        'kv_norm': kv_norm[None, :], 'w_k': w_kv[:, :dk].astype(BF16), 'w_v': w_kv[:, dk:].astype(BF16),
        'k_norm': jnp.tile(k_norm, KV)[None, :],
        'w_q': wq_slot, 'q_norm': row1(jnp.tile(q_norm, (1, nh))),
        'sinks_row': row1(jnp.pad(sinks_slot, ((0, 0), (0, LANES - nh)))),
        'sinks_col': sinks_slot[:, :, None],
        'w_o': wo_slot,
    }
    bs = x_sample.shape[0]
    y_p, ssm_p, conv_p, k_p, v_p = _trunk(x_prompt, p_prompt, None, None, None, None, 0, w, dims, True)
    y_s, ssm_s, conv_s, k_s, v_s = _trunk(x_sample, p_sample, state_ssm.reshape(n_a, bs, heads * P, N),
                                          state_conv.reshape(n_a, bs, (CONV_TAPS - 1) * cd),
                                          cache_k.reshape(bs, W, dk), cache_v.reshape(bs, W, dk), PAST_LEN, w, dims, False)
    return (y_p, y_s, ssm_p, conv_p, k_p, v_p, ssm_s, conv_s, k_s, v_s)
```

```python
import functools

import numpy as np
import jax
import jax.numpy as jnp
from jax import lax
from jax.experimental import pallas as pl
from jax.experimental.pallas import tpu as pltpu

F32 = jnp.float32
BF16 = jnp.bfloat16
EPS = 1e-6
PAST_LEN = 8192
ROPE_THETA = 10000.0
LOG2E = 1.4426950408889634
LANES = 128
SUBLANES = 8
SSD_CHUNK = 128
CONV_TAPS = 4
VMEM_V7X_BYTES = 64 * 1024 * 1024
VMEM_BYTES = VMEM_V7X_BYTES - 8 * 1024 * 1024

TOKEN_TILE = 512
SPLIT_ROWS = 512
FFN_CHUNK = 256
CONV_SLABS = 4
SSD_TILE = 512
ATTN_TILE = 1024
STATE_SEQS = 4
ATTN_STEP_SEQS = 16


def _dot(a, b):
    return jnp.dot(a, b, preferred_element_type=F32)


def _dot_nt(a, b):
    return lax.dot_general(a, b, (((1,), (1,)), ((), ())), preferred_element_type=F32)


def _split(a, parts):
    pieces = []
    for _ in range(parts):
        piece = a.astype(BF16)
        pieces.append(piece)
        a = a - piece.astype(F32)
    return pieces


def _dot_split_lhs(a, b_stacked, parts=2):
    return _dot(jnp.concatenate(_split(a, parts), axis=1), b_stacked)


def _dot_split_rhs(a_tiled, b, parts=2):
    return _dot(a_tiled, jnp.concatenate(_split(b, parts), axis=0))


def _rms(x, g):
    ms = jnp.mean(x * x, axis=-1, keepdims=True)
    return x * lax.rsqrt(ms + EPS) * g


def _sigmoid(x):
    return 0.5 * jnp.tanh(0.5 * x) + 0.5


def _silu(x):
    half = 0.5 * x
    return half * jnp.tanh(half) + half


def _softplus(v):
    return jnp.maximum(v, 0.0) + jnp.log1p(jnp.exp(-jnp.abs(v)))


def _seg_rms(x, seg_sum, seg_exp, seg):
    ms = _dot((x * x).astype(BF16), seg_sum) * (1.0 / seg)
    return _dot_split_lhs(lax.rsqrt(ms + EPS), seg_exp)


def _rope(x, cos, sin_signed, half):
    width = x.shape[-1]
    lane = lax.broadcasted_iota(jnp.int32, x.shape, 1)
    first = (lane % (2 * half)) < half
    rot = jnp.where(first, pltpu.roll(x, width - half, 1), pltpu.roll(x, half, 1))
    return x * cos + rot * sin_signed


def _const_spec(shape):
    zeros = (0,) * len(shape)
    return pl.BlockSpec(shape, lambda *_: zeros, pipeline_mode=pl.Buffered(1))


def _params(sem):
    return pltpu.CompilerParams(dimension_semantics=sem, vmem_limit_bytes=VMEM_BYTES)


def _seg_mats(width, seg):
    s = np.zeros((width, LANES), np.float32)
    s[np.arange(width), np.arange(width) // seg] = 1.0
    return jnp.asarray(s, BF16), jnp.asarray(np.tile(s.T, (2, 1)), BF16)


def _head_expand_mat(heads, p, parts):
    e = np.zeros((LANES, heads * p), np.float32)
    e[np.arange(heads * p) // p, np.arange(heads * p)] = 1.0
    return jnp.asarray(np.tile(e, (parts, 1)), BF16)


def _half_swiglu(x, g_ref, wi_ref, wo_ref, h_scr, dff, fc, rs=slice(None)):
    xn = _rms(x, g_ref[...]).astype(BF16)
    for c in range(dff // fc):
        gt = _dot(xn, wi_ref[:, c * fc:(c + 1) * fc])
        up = _dot(xn, wi_ref[:, dff + c * fc:dff + (c + 1) * fc])
        h_scr[rs, c * fc:(c + 1) * fc] = (_silu(gt) * up).astype(BF16)
    return x + 0.5 * _dot(h_scr[rs, :], wo_ref[...])


def _normed_rope(v, gain_ref, ssum_ref, sexp_ref, cos_ref, sin_ref, hd, rs):
    vn = v * _seg_rms(v, ssum_ref[...], sexp_ref[...], hd) * gain_ref[...]
    reps = v.shape[1] // LANES
    return _rope(vn, jnp.tile(cos_ref[rs, :], (1, reps)), jnp.tile(sin_ref[rs, :], (1, reps)), hd // 2)


def _layer_spec(arr, layer):
    shape = tuple(arr.shape[1:])
    index = (layer,) + (0,) * len(shape)
    return pl.BlockSpec((None,) + shape, lambda *_: index, pipeline_mode=pl.Buffered(1))


def _pre_kernel(*refs, with_q, dff, fc, hd, scale):
    x_ref, g_ref, wi_ref, wo_ref = refs[:4]
    if with_q:
        mg_ref, wq_ref, qg_ref, ssum_ref, sexp_ref, cos_ref, sin_ref, o_ref, q_ref, h_scr = refs[4:]
    else:
        o_ref, h_scr = refs[4:]
    tm = x_ref.shape[0]
    nsplit = 2 if (with_q and tm % SPLIT_ROWS == 0) else 1
    for sp in range(nsplit):
        rs = slice(sp * (tm // nsplit), (sp + 1) * (tm // nsplit))
        x1 = _half_swiglu(x_ref[rs, :], g_ref, wi_ref, wo_ref, h_scr, dff, fc, rs)
        o_ref[rs, :] = x1
        if with_q:
            q = _dot(_rms(x1, mg_ref[...]).astype(BF16), wq_ref[...])
            qr = _normed_rope(q, qg_ref, ssum_ref, sexp_ref, cos_ref, sin_ref, hd, rs)
            q_ref[rs, :] = (qr * scale).astype(q_ref.dtype)


def _pre(x, layer, g, wi, wo, tm, qargs=None, q_dtype=BF16):
    m, d = x.shape
    dff = wo.shape[1]
    fc = FFN_CHUNK if dff % FFN_CHUNK == 0 else dff
    row = lambda w: pl.BlockSpec((tm, w), lambda i: (i, 0))
    in_specs = [row(d), _layer_spec(g, layer), _layer_spec(wi, layer), _layer_spec(wo, layer)]
    args = [x, g, wi, wo]
    out_specs, out_shape = [row(d)], [jax.ShapeDtypeStruct((m, d), F32)]
    hd = 0
    if qargs is not None:
        mg, wq, qg, qlayer, cos, sin, hd = qargs
        dq = wq.shape[2]
        ssum, sexp = _seg_mats(dq, hd)
        nt = cos.shape[0] // tm
        tab = pl.BlockSpec((tm, LANES), lambda i: (i % nt, 0))
        in_specs += [_layer_spec(mg, layer), _layer_spec(wq, qlayer), _layer_spec(qg, qlayer),
                     _const_spec(ssum.shape), _const_spec(sexp.shape), tab, tab]
        args += [mg, wq, qg, ssum, sexp, cos, sin]
        out_specs.append(row(dq))
        out_shape.append(jax.ShapeDtypeStruct((m, dq), q_dtype))
    return pl.pallas_call(
        functools.partial(_pre_kernel, with_q=qargs is not None, dff=dff, fc=fc, hd=hd,
                          scale=hd ** -0.5 * LOG2E if hd else 1.0),
        grid=(m // tm,),
        in_specs=in_specs, out_specs=out_specs, out_shape=out_shape,
        scratch_shapes=[pltpu.VMEM((tm, dff), BF16)],
        compiler_params=_params(("parallel",)),
        name="pre",
    )(*args)


def _post_kernel(*refs, mamba, with_kv, dff, fc, gw, hd):
    it = iter(refs)
    x_ref = next(it)
    if mamba:
        y_ref, gate_ref, ng_ref, ssum_ref, sexp_ref, wmix_ref = [next(it) for _ in range(6)]
    else:
        att_ref, wmix_ref = next(it), next(it)
    g2_ref, wi_ref, wo_ref, pg_ref, wg_ref, p_ref, wp_ref = [next(it) for _ in range(7)]
    if with_kv:
        kvg_ref, wk_ref, wv_ref, kg_ref, ksum_ref, kexp_ref, cos_ref, sin_ref = [next(it) for _ in range(8)]
    o_ref = next(it)
    if with_kv:
        k_ref, v_ref = next(it), next(it)
    h_scr = next(it)

    tm = x_ref.shape[0]
    nsplit = 2 if (mamba and tm % SPLIT_ROWS == 0) else 1
    for sp in range(nsplit):
        rs = slice(sp * (tm // nsplit), (sp + 1) * (tm // nsplit))
        if mamba:
            gated = y_ref[rs, :].astype(F32) * gate_ref[rs, :].astype(F32)
            r = _seg_rms(gated, ssum_ref[...], sexp_ref[...], gw)
            mix = _dot((gated * r * ng_ref[...]).astype(BF16), wmix_ref[...])
        else:
            mix = _dot(att_ref[rs, :].astype(BF16), wmix_ref[...])
        x = _half_swiglu(x_ref[rs, :] + mix, g2_ref, wi_ref, wo_ref, h_scr, dff, fc, rs)
        gate = _sigmoid(_dot(_rms(x, pg_ref[...]).astype(BF16), wg_ref[...]))
        x = x + gate * _dot(p_ref[rs, :].astype(BF16), wp_ref[...])
        o_ref[rs, :] = x
        if with_kv:
            xn = _rms(x, kvg_ref[...]).astype(BF16)
            v_ref[rs, :] = _dot(xn, wv_ref[...])
            k_ref[rs, :] = _normed_rope(_dot(xn, wk_ref[...]), kg_ref, ksum_ref, kexp_ref, cos_ref, sin_ref, hd, rs)


def _post(x, layer, mix_args, ffn_args, ple_args, tm, kv_args=None):
    m, d = x.shape
    g2, wi, wo = ffn_args
    pg, wg, p, wp = ple_args
    dff = wo.shape[1]
    fc = FFN_CHUNK if dff % FFN_CHUNK == 0 else dff
    dp = p.shape[2]
    row = lambda w: pl.BlockSpec((tm, w), lambda i: (i, 0))
    in_specs, args = [row(d)], [x]
    mamba = len(mix_args) == 6
    gw = hd = 0
    if mamba:
        y, gate, ng, wout, mlayer, groups = mix_args
        di = y.shape[1]
        gw = di // groups
        ssum, sexp = _seg_mats(di, gw)
        in_specs += [row(di), row(di), _layer_spec(ng, mlayer), _const_spec(ssum.shape), _const_spec(sexp.shape),
                     _layer_spec(wout, mlayer)]
        args += [y, gate, ng, ssum, sexp, wout]
    else:
        att, wmix, mlayer = mix_args
        in_specs += [row(att.shape[1]), _layer_spec(wmix, mlayer)]
        args += [att, wmix]
    in_specs += [_layer_spec(g2, layer), _layer_spec(wi, layer), _layer_spec(wo, layer),
                 _layer_spec(pg, layer), _layer_spec(wg, layer),
                 pl.BlockSpec((None, tm, dp), lambda i: (layer, i, 0)), _layer_spec(wp, layer)]
    args += [g2, wi, wo, pg, wg, p, wp]
    out_specs, out_shape = [row(d)], [jax.ShapeDtypeStruct((m, d), F32)]
    if kv_args is not None:
        kvg, wk, wv, kg, cos, sin, hd = kv_args
        dk = wk.shape[1]
        ksum, kexp = _seg_mats(dk, hd)
        nt = cos.shape[0] // tm
        tab = pl.BlockSpec((tm, LANES), lambda i: (i % nt, 0))
        in_specs += [_const_spec((1, d)), _const_spec((d, dk)), _const_spec((d, dk)), _const_spec((1, dk)),
                     _const_spec(ksum.shape), _const_spec(kexp.shape), tab, tab]
        args += [kvg, wk, wv, kg, ksum, kexp, cos, sin]
        out_specs += [row(dk), row(dk)]
        out_shape += [jax.ShapeDtypeStruct((m, dk), F32)] * 2
    return pl.pallas_call(
        functools.partial(_post_kernel, mamba=mamba, with_kv=kv_args is not None, dff=dff, fc=fc, gw=gw, hd=hd),
        grid=(m // tm,),
        in_specs=in_specs, out_specs=out_specs, out_shape=out_shape,
        scratch_shapes=[pltpu.VMEM((tm, dff), BF16)],
        compiler_params=_params(("parallel",)),
        name="post",
    )(*args)


def _mamba_in_kernel(x_ref, g_ref, w_ref, wd_ref, gate_ref, xbc_ref, dt_ref, *, nc, di):
    xn = _rms(x_ref[...], g_ref[...]).astype(BF16)
    gate_ref[...] = _silu(_dot(xn, w_ref[:, :di])).astype(BF16)
    cw = xbc_ref.shape[1] // nc
    for c in range(nc):
        xbc_ref[:, c * cw:(c + 1) * cw] = _dot(xn, w_ref[:, di + c * cw:di + (c + 1) * cw])
    dt_ref[...] = _dot(xn, wd_ref[...])


def _mamba_in(x, layer, g, w_in, wd, di, cd, tm):
    m, d = x.shape
    row = lambda w: pl.BlockSpec((tm, w), lambda i: (i, 0))
    return pl.pallas_call(
        functools.partial(_mamba_in_kernel, nc=CONV_SLABS, di=di),
        grid=(m // tm,),
        in_specs=[row(d), _layer_spec(g, layer), _layer_spec(w_in, layer), _layer_spec(wd, layer)],
        out_specs=[row(di), row(cd), row(LANES)],
        out_shape=[jax.ShapeDtypeStruct((m, di), BF16), jax.ShapeDtypeStruct((m, cd), F32),
                   jax.ShapeDtypeStruct((m, LANES), F32)],
        compiler_params=_params(("parallel",)),
        name="mamba_in",
    )(x, g, w_in, wd)


def _mamba_in_conv_kernel(x_ref, g_ref, w_ref, wd_ref, cw_ref, cb_ref,
                          gate_ref, act_ref, dt_ref, tail_ref, buf, carry, *, tm, L, nc, di):
    t = pl.program_id(1)
    xn = _rms(x_ref[...], g_ref[...]).astype(BF16)
    gate_ref[...] = _silu(_dot(xn, w_ref[:, :di])).astype(BF16)
    dt_ref[...] = _dot(xn, wd_ref[...])
    cw = act_ref.shape[1] // nc

    @pl.when(t == 0)
    def _():
        carry[...] = jnp.zeros_like(carry)

    for c in range(nc):
        cs = slice(c * cw, (c + 1) * cw)
        buf[0:SUBLANES, :] = carry[:, cs]
        buf[SUBLANES:SUBLANES + tm, :] = _dot(xn, w_ref[:, di + c * cw:di + (c + 1) * cw])
        for r in range(tm // L):
            base = SUBLANES + r * L
            conv = cb_ref[:, cs]
            for k in reversed(range(CONV_TAPS)):
                lo = base - (CONV_TAPS - 1) + k
                conv = conv + cw_ref[k:k + 1, cs] * buf[lo:lo + L, :]
            act_ref[r * L:(r + 1) * L, cs] = _silu(conv).astype(BF16)
        last = buf[tm:tm + SUBLANES, :]
        carry[:, cs] = last
        tail_ref[0, :, cs] = last


def _mamba_in_conv(x, layer, g, w_in, wd, cw, cb, di, cd, bsz, tm):
    m, d = x.shape
    nt = m // bsz // tm
    nc = CONV_SLABS
    row = lambda w: pl.BlockSpec((tm, w), lambda b, t: (b * nt + t, 0))
    return pl.pallas_call(
        functools.partial(_mamba_in_conv_kernel, tm=tm, L=min(SSD_CHUNK, tm), nc=nc, di=di),
        grid=(bsz, nt),
        in_specs=[row(d), _layer_spec(g, layer), _layer_spec(w_in, layer), _layer_spec(wd, layer),
                  _layer_spec(cw, layer), _layer_spec(cb, layer)],
        out_specs=[row(di), row(cd), row(LANES), pl.BlockSpec((1, SUBLANES, cd), lambda b, t: (b, 0, 0))],
        out_shape=[jax.ShapeDtypeStruct((m, di), BF16), jax.ShapeDtypeStruct((m, cd), BF16),
                   jax.ShapeDtypeStruct((m, LANES), F32), jax.ShapeDtypeStruct((bsz, SUBLANES, cd), F32)],
        scratch_shapes=[pltpu.VMEM((tm + SUBLANES, cd // nc), F32), pltpu.VMEM((SUBLANES, cd), F32)],
        compiler_params=_params(("parallel", "arbitrary")),
        name="mamba_in_conv",
    )(x, g, w_in, wd, cw, cb)


def _ssd_kernel(act_ref, dt_ref, dtb_ref, alog_ref, dsk_ref, tril_ref, e_ref,
                y_ref, h_ref, *, tt, L, G, EH, P, N):
    t = pl.program_id(1)
    di = G * EH * P
    gw = EH * P

    @pl.when(t == 0)
    def _():
        h_ref[...] = jnp.zeros_like(h_ref)

    a = -jnp.exp(alog_ref[...]) * LOG2E
    tril = tril_ref[...]
    emat = e_ref[...]
    rows = lax.broadcasted_iota(jnp.int32, (L, L), 0)
    cols = lax.broadcasted_iota(jnp.int32, (L, L), 1)
    causal = rows >= cols
    lane = lax.broadcasted_iota(jnp.int32, (L, gw), 1)
    head_masks = [(lane // P) == e for e in range(EH)]

    for c in range(tt // L):
        rs = slice(c * L, (c + 1) * L)
        dt = _softplus(dt_ref[rs, :] + dtb_ref[...])
        cum = _dot_split_rhs(tril, dt * a, parts=3)
        cum_t = cum.T
        dt_t = dt.T
        to_end = jnp.exp2(cum[L - 1:L, :] - cum)
        expanded = _dot_split_lhs(jnp.concatenate([dt * to_end, jnp.exp2(cum)], axis=0), emat)
        xw_t = (act_ref[rs, :di].astype(F32) * expanded[:L]).T.astype(BF16)
        chunk_decay = jnp.broadcast_to(jnp.exp2(cum_t[:, L - 1:L]), (LANES, N))

        for g in range(G):
            bg = act_ref[rs, di + g * N:di + (g + 1) * N]
            cg = act_ref[rs, di + (G + g) * N:di + (G + g + 1) * N]
            cb = _dot_nt(cg, bg)
            sl = slice(g * gw, (g + 1) * gw)
            xs_g = act_ref[rs, sl]
            mats, blocks = [], []
            for e in range(EH):
                h = g * EH + e
                seg = cum[:, h:h + 1] - cum_t[h:h + 1, :]
                decay = jnp.exp2(jnp.where(causal, seg, -jnp.inf))
                mats.append((cb * decay * dt_t[h:h + 1, :]).astype(BF16))
                blocks.append(jnp.where(head_masks[e], xs_g, jnp.zeros_like(xs_g)))
            y_intra = _dot(jnp.concatenate(mats, axis=1), jnp.concatenate(blocks, axis=0))
            hg = h_ref[0, sl, :]
            y_inter = _dot_nt(cg, hg.astype(BF16)) * expanded[L:, sl]
            y_ref[rs, sl] = (y_intra + y_inter + dsk_ref[:, sl] * xs_g.astype(F32)).astype(y_ref.dtype)
            s_new = _dot(xw_t[sl, :], bg)
            for e in range(EH):
                h = g * EH + e
                hs = slice(e * P, (e + 1) * P)
                h_ref[0, g * gw + e * P:g * gw + (e + 1) * P, :] = hg[hs, :] * chunk_decay[h:h + 1, :] + s_new[hs, :]


def _ssd(act, dt_raw, layer, dtb, alog, dsk, bsz, G, EH, P, N, tt):
    m, cd = act.shape
    nt = m // bsz // tt
    L = min(SSD_CHUNK, tt)
    heads = G * EH
    di = heads * P
    tril = jnp.asarray(np.tile(np.tril(np.ones((L, L), np.float32)), (1, 3)), BF16)
    emat = _head_expand_mat(heads, P, 2)
    row = lambda w: pl.BlockSpec((tt, w), lambda b, t: (b * nt + t, 0))
    return pl.pallas_call(
        functools.partial(_ssd_kernel, tt=tt, L=L, G=G, EH=EH, P=P, N=N),
        grid=(bsz, nt),
        in_specs=[row(cd), row(LANES), _layer_spec(dtb, layer), _layer_spec(alog, layer), _layer_spec(dsk, layer),
                  _const_spec(tril.shape), _const_spec(emat.shape)],
        out_specs=[row(di), pl.BlockSpec((1, di, N), lambda b, t: (b, 0, 0))],
        out_shape=[jax.ShapeDtypeStruct((m, di), BF16), jax.ShapeDtypeStruct((bsz, di, N), F32)],
        compiler_params=_params(("parallel", "arbitrary")),
        name="ssd",
    )(act, dt_raw, dtb, alog, dsk, tril, emat)


def _ssm_step_kernel(*refs, bb, G, EH, P, N, layer, aliased):
    (h0_ref, xbc_ref, c0_ref, dt_ref, cw_ref, cb_ref, dtb_ref, alog_ref, dsk_ref, e_ref) = refs[:10]
    rest = refs[11:] if aliased else refs[10:]
    hn_all, y_ref, cnew_ref, xdt_t_scr, dec_t_scr, b_scr, c_scr, xs_scr = rest
    if aliased:
        hn_ref = hn_all
    else:
        hn_ref = hn_all.at[layer]
        for other in range(hn_all.shape[0]):
            if other != layer:
                hn_all[other] = jnp.zeros(hn_all.shape[1:], F32)
    i = pl.program_id(0)
    cd = xbc_ref.shape[1]
    di = G * EH * P
    gw = EH * P
    db = xbc_ref.shape[0]

    @pl.when(i == 0)
    def _():
        x = xbc_ref[...]
        conv = cb_ref[...] + cw_ref[CONV_TAPS - 1:CONV_TAPS, :] * x
        for k in range(CONV_TAPS - 1):
            conv = conv + cw_ref[k:k + 1, :] * c0_ref[:, k * cd:(k + 1) * cd]
        cnew_ref[:, 0:(CONV_TAPS - 2) * cd] = c0_ref[:, cd:(CONV_TAPS - 1) * cd]
        cnew_ref[:, (CONV_TAPS - 2) * cd:(CONV_TAPS - 1) * cd] = x
        act = _silu(conv)
        xs = act[:, :di]
        xs_scr[...] = xs
        b_scr[...] = act[:, di:di + G * N]
        c_scr[...] = act[:, di + G * N:]
        dt = _softplus(dt_ref[...] + dtb_ref[...])
        decay = jnp.exp(dt * (-jnp.exp(alog_ref[...])))
        xdt_t_scr[...] = (xs * _dot_split_lhs(dt, e_ref[...], parts=3)).T
        dec_t_scr[...] = decay.T

    seq = lax.broadcasted_iota(jnp.int32, (1, db), 1)
    for j in range(bb):
        b = i * bb + j
        pick = seq == b
        xcol = jnp.sum(jnp.where(pick, xdt_t_scr[...], 0.0), axis=1, keepdims=True)
        dcol = jnp.sum(jnp.where(pick, dec_t_scr[...], 0.0), axis=1, keepdims=True)
        brow = b_scr[pl.ds(b, 1), :]
        crow = c_scr[pl.ds(b, 1), :]
        parts = []
        for g in range(G):
            bg = brow[:, g * N:(g + 1) * N]
            pieces = []
            for e in range(EH):
                h = g * EH + e
                rows = slice(h * P, (h + 1) * P)
                pieces.append(h0_ref[j, rows, :] * dcol[h:h + 1, :] + xcol[rows, :] * bg)
            hn = jnp.concatenate(pieces, axis=0)
            hn_ref[j, g * gw:(g + 1) * gw, :] = hn
            c8 = jnp.broadcast_to(crow[:, g * N:(g + 1) * N], (SUBLANES, N)).astype(BF16)
            parts.append(_dot_nt(c8, hn.astype(BF16))[0:1, :])
        y_ref[pl.ds(b, 1), :] = jnp.concatenate(parts, axis=1) + dsk_ref[...] * xs_scr[pl.ds(b, 1), :]


def _ssm_step(h_all, layer, h_prev_out, xbc, c0, dt_raw, cw, cb, dtb, alog, dsk, G, EH, P, N, bb):
    db, cd = xbc.shape
    hp = G * EH * P
    di = hp
    emat = _head_expand_mat(G * EH, P, 3)
    st = pl.BlockSpec((None, bb, hp, N), lambda i: (layer, i, 0, 0))
    st_all = pl.BlockSpec((h_all.shape[0], bb, hp, N), lambda i: (0, i, 0, 0))
    taps = CONV_TAPS - 1
    in_specs = [st, _const_spec((db, cd)), _layer_spec(c0, layer), _const_spec((db, LANES)),
                _layer_spec(cw, layer), _layer_spec(cb, layer), _layer_spec(dtb, layer), _layer_spec(alog, layer),
                _layer_spec(dsk, layer), _const_spec(emat.shape)]
    args = [h_all, xbc, c0, dt_raw, cw, cb, dtb, alog, dsk, emat]
    aliases = {}
    if h_prev_out is not None:
        in_specs.append(pl.BlockSpec(memory_space=pl.ANY))
        args.append(h_prev_out)
        aliases = {len(args) - 1: 0}
    return pl.pallas_call(
        functools.partial(_ssm_step_kernel, bb=bb, G=G, EH=EH, P=P, N=N, layer=layer, aliased=h_prev_out is not None),
        grid=(db // bb,),
        in_specs=in_specs,
        out_specs=[st if h_prev_out is not None else st_all,
                   pl.BlockSpec((db, di), lambda i: (0, 0)), pl.BlockSpec((db, taps * cd), lambda i: (0, 0))],
        out_shape=[jax.ShapeDtypeStruct(h_all.shape, F32), jax.ShapeDtypeStruct((db, di), F32),
                   jax.ShapeDtypeStruct((db, taps * cd), F32)],
        scratch_shapes=[pltpu.VMEM((hp, db), F32), pltpu.VMEM((LANES, db), F32),
                        pltpu.VMEM((db, G * N), F32), pltpu.VMEM((db, G * N), F32), pltpu.VMEM((db, di), F32)],
        input_output_aliases=aliases,
        compiler_params=_params(("arbitrary",)),
        name="ssm_step",
    )(*args)


def _attn_kernel(q_ref, kh_ref, km_ref, vh_ref, vm_ref, sink_ref, o_ref, *, tq, W, KV, GQ, hd):
    t = pl.program_id(1)
    per = LANES // hd
    key = lax.broadcasted_iota(jnp.int32, (W, W), 0)
    r = lax.broadcasted_iota(jnp.int32, (W, W), 1)
    cur_ok = key <= r
    half_k = lax.broadcasted_iota(jnp.int32, (2 * W, LANES), 1) // hd
    half_o = lax.broadcasted_iota(jnp.int32, (LANES, W), 0) // hd
    nblk = tq // W
    for j in range(nblk):
        first = jnp.logical_and(t == 0, j == 0)
        prev_ok = key >= jnp.maximum(r, jnp.where(first, W, 0))
        rows_cur = slice(j * W, (j + 1) * W)
        if j == 0:
            k2 = jnp.concatenate([kh_ref[...], km_ref[rows_cur, :]], axis=0)
            v2 = jnp.concatenate([vh_ref[...], vm_ref[rows_cur, :]], axis=0)
        else:
            k2 = km_ref[(j - 1) * W:(j + 1) * W, :]
            v2 = vm_ref[(j - 1) * W:(j + 1) * W, :]
        for blk in range(KV // per):
            ls = slice(blk * LANES, (blk + 1) * LANES)
            kblk = k2[:, ls]
            k_u = [jnp.where(half_k == u, kblk, 0.0).astype(BF16) for u in range(per)]
            v_t = v2[:, ls].T.astype(BF16)
            for e in range(GQ):
                cb = (e * (KV // per) + blk) * LANES
                qs = q_ref[j * W:(j + 1) * W, cb:cb + LANES]
                acc_t = None
                for u in range(per):
                    slot = e * KV + blk * per + u
                    sink = sink_ref[:, slot:slot + 1] * LOG2E
                    s_t = _dot_nt(k_u[u], qs)
                    s_prev = jnp.where(prev_ok, s_t[:W], -jnp.inf)
                    s_cur = jnp.where(cur_ok, s_t[W:], -jnp.inf)
                    mx = jnp.maximum(jnp.max(jnp.maximum(s_prev, s_cur), axis=0, keepdims=True), sink)
                    p_prev = jnp.exp2(s_prev - mx)
                    p_cur = jnp.exp2(s_cur - mx)
                    den = jnp.sum(p_prev + p_cur, axis=0, keepdims=True) + jnp.exp2(sink - mx)
                    o_t = _dot(v_t, jnp.concatenate([p_prev, p_cur], axis=0).astype(BF16)) / den
                    acc_t = o_t if acc_t is None else jnp.where(half_o == u, o_t, acc_t)
                o_ref[j * W:(j + 1) * W, cb:cb + LANES] = acc_t.T.astype(o_ref.dtype)


def _attn(q, k, v, sinks, layer, bsz, W, KV, GQ, hd, tq):
    m, dq = q.shape
    nt = m // bsz // tq
    dk = KV * hd
    per_tile = tq // W
    row = lambda w: pl.BlockSpec((tq, w), lambda b, t: (b * nt + t, 0))
    head = pl.BlockSpec((W, dk), lambda b, t: (jnp.maximum((b * nt + t) * per_tile - 1, 0), 0))
    return pl.pallas_call(
        functools.partial(_attn_kernel, tq=tq, W=W, KV=KV, GQ=GQ, hd=hd),
        grid=(bsz, nt),
        in_specs=[row(dq), head, row(dk), head, row(dk), _layer_spec(sinks, layer)],
        out_specs=row(dq),
        out_shape=jax.ShapeDtypeStruct((m, dq), BF16),
        compiler_params=_params(("parallel", "parallel")),
        name="attn",
    )(q, k, k, v, v, sinks)


def _attn_step_kernel(q_ref, kc_ref, vc_ref, kn_ref, vn_ref, sink_ref, o_ref, *, bb, KV, GQ, hd):
    nh = KV * GQ
    dq = nh * hd
    dk = KV * hd
    lane_q = lax.broadcasted_iota(jnp.int32, (nh, dq), 1)
    row_q = lax.broadcasted_iota(jnp.int32, (nh, dq), 0)
    own_q = (lane_q // hd) == row_q
    lane_k = lax.broadcasted_iota(jnp.int32, (nh, dk), 1)
    row_k = lax.broadcasted_iota(jnp.int32, (nh, dk), 0)
    own_k = (lane_k // hd) == (row_k % KV)
    sink = sink_ref[...] * LOG2E
    for j in range(bb):
        qm = jnp.where(own_q, jnp.broadcast_to(q_ref[j:j + 1, :], (nh, dq)), 0.0)
        qbd = qm[:, 0:dk]
        for e in range(1, GQ):
            qbd = qbd + qm[:, e * dk:(e + 1) * dk]
        s = _dot_nt(qbd.astype(BF16), kc_ref[j].astype(BF16))
        s_new = jnp.sum(qbd * kn_ref[j:j + 1, :], axis=-1, keepdims=True)
        mx = jnp.maximum(jnp.maximum(jnp.max(s, axis=-1, keepdims=True), s_new), sink)
        p = jnp.exp2(s - mx)
        p_new = jnp.exp2(s_new - mx)
        den = jnp.sum(p, axis=-1, keepdims=True) + p_new + jnp.exp2(sink - mx)
        o = (_dot(p.astype(BF16), vc_ref[j].astype(BF16)) + p_new * vn_ref[j:j + 1, :]) / den
        om = jnp.where(own_k, o, 0.0)
        pieces = [jnp.sum(om[e * KV:(e + 1) * KV, :], axis=0, keepdims=True) for e in range(GQ)]
        o_ref[j:j + 1, :] = jnp.concatenate(pieces, axis=1)


def _attn_step(q, kc, vc, kn, vn, sinks_col, layer, KV, GQ, hd, bb):
    db, dq = q.shape
    W, dk = kc.shape[1], kc.shape[2]
    cache = pl.BlockSpec((bb, W, dk), lambda i: (i, 0, 0))
    rows = lambda w: pl.BlockSpec((bb, w), lambda i: (i, 0))
    return pl.pallas_call(
        functools.partial(_attn_step_kernel, bb=bb, KV=KV, GQ=GQ, hd=hd),
        grid=(db // bb,),
        in_specs=[rows(dq), cache, cache, rows(dk), rows(dk), _layer_spec(sinks_col, layer)],
        out_specs=rows(dq),
        out_shape=jax.ShapeDtypeStruct((db, dq), F32),
        compiler_params=_params(("parallel",)),
        name="attn_step",
    )(q, kc, vc, kn, vn, sinks_col)


def _rope_tables(pos, hd):
    half = hd // 2
    inv = ROPE_THETA ** (-jnp.arange(half, dtype=F32) / half)
    ang = pos.astype(F32)[:, None] * inv[None, :]
    cos, sin = jnp.cos(ang), jnp.sin(ang)
    reps = LANES // hd
    return (jnp.tile(jnp.concatenate([cos, cos], axis=1), (1, reps)),
            jnp.tile(jnp.concatenate([-sin, sin], axis=1), (1, reps)))


def _trunk(x, p, ssm0, conv0, k_hist, v_hist, pos0, w, dims, is_prompt):
    bsz, t_len, d = x.shape
    m = bsz * t_len
    depth, n_a, G, EH, P, N, KV, GQ, hd, W = dims
    di = G * EH * P
    cd = di + 2 * G * N
    dk = KV * hd
    tm = TOKEN_TILE if t_len % TOKEN_TILE == 0 else m
    x = x.reshape(m, d)
    p = p.reshape(depth, m, -1)
    pos = pos0 + (jnp.arange(t_len) if is_prompt else jnp.zeros((m,), jnp.int32))
    cos, sin = _rope_tables(pos, hd)
    ffn1 = (w['ffn1_norm'], w['ffn1_wi'], w['ffn1_wo'])
    new_conv = []
    ssm_out = None
    k_new = v_new = None
    mix_args = None
    for i in range(depth):
        if i < n_a:
            win = (w['mix_norm'], w['ssm_in'], w['ssm_wd'])
            sargs = (w['ssm_dt_bias'], w['ssm_a_log'], w['ssm_dsk'])
            x = _pre(x, i, *ffn1, tm)[0]
            if is_prompt:
                gate, act, dt_raw, tail = _mamba_in_conv(x, i, *win, w['ssm_conv_w'], w['ssm_conv_b'], di, cd, bsz, tm)
                y, h_last = _ssd(act, dt_raw, i, *sargs, bsz, G, EH, P, N, tt=min(SSD_TILE, t_len))
                ssm_out = h_last[None] if ssm_out is None else jnp.concatenate([ssm_out, h_last[None]], axis=0)
                new_conv.append(tail[:, SUBLANES - (CONV_TAPS - 1):])
            else:
                gate, xbc, dt_raw = _mamba_in(x, i, *win, di, cd, tm)
                ssm_out, y, c_last = _ssm_step(ssm0, i, ssm_out, xbc, conv0, dt_raw,
                                               w['ssm_conv_w'], w['ssm_conv_b'], *sargs, G, EH, P, N, bb=STATE_SEQS)
                new_conv.append(c_last.reshape(bsz, CONV_TAPS - 1, cd))
            mix_args = (y, gate, w['ssm_norm'], w['ssm_out'], i, G)
        else:
            j = i - n_a
            qargs = (w['mix_norm'], w['w_q'], w['q_norm'], j, cos, sin, hd)
            x, q = _pre(x, i, *ffn1, tm, qargs, q_dtype=BF16 if is_prompt else F32)
            if is_prompt:
                o = _attn(q, k_new, v_new, w['sinks_row'], j, bsz, W, KV, GQ, hd, tq=min(ATTN_TILE, t_len))
            else:
                o = _attn_step(q, k_hist, v_hist, k_new, v_new, w['sinks_col'], j, KV, GQ, hd, bb=ATTN_STEP_SEQS)
            mix_args = (o, w['w_o'], j)
        kv_args = (w['kv_norm'], w['w_k'], w['w_v'], w['k_norm'], cos, sin, hd) if i == n_a - 1 else None
        res = _post(x, i, mix_args, (w['ffn2_norm'], w['ffn2_wi'], w['ffn2_wo']),
                    (w['ple_norm'], w['ple_gate'], p, w['ple_proj']), tm, kv_args)
        x = res[0]
        if kv_args is not None:
            k_new, v_new = res[1], res[2]
    if is_prompt:
        k_out = k_new.reshape(bsz, t_len, dk)[:, t_len - W:]
        v_out = v_new.reshape(bsz, t_len, dk)[:, t_len - W:]
    else:
        k_out = jnp.concatenate([k_hist, k_new.reshape(bsz, t_len, dk)], axis=1)[:, -W:]
        v_out = jnp.concatenate([v_hist, v_new.reshape(bsz, t_len, dk)], axis=1)[:, -W:]
    return (x.reshape(bsz, t_len, d), ssm_out.reshape(n_a, bsz, G * EH, P, N), jnp.stack(new_conv),
            k_out.reshape(bsz, W, KV, hd), v_out.reshape(bsz, W, KV, hd))


def kernel(x_prompt, x_sample, state_ssm, state_conv, cache_k, cache_v, p_prompt, p_sample, ffn1_norm, ffn1_wi, ffn1_wo, mix_norm, ffn2_norm, ffn2_wi, ffn2_wo, ple_norm, ple_gate, ple_proj, ssm_in, ssm_conv_w, ssm_conv_b, ssm_dt_bias, ssm_a_log, ssm_d, ssm_norm, ssm_out, kv_norm, w_kv, k_norm, w_q, q_norm, attn_sinks, w_o):
    depth, d = ffn1_norm.shape
    n_a, heads = ssm_a_log.shape
    di = ssm_norm.shape[1]
    cd = ssm_conv_b.shape[1]
    N = state_ssm.shape[-1]
    P = state_ssm.shape[-2]
    G = (cd - di) // (2 * N)
    EH = heads // G
    W, KV, hd = cache_k.shape[1], cache_k.shape[2], cache_k.shape[3]
    nh = w_q.shape[2] // hd
    GQ = nh // KV
    dims = (depth, n_a, G, EH, P, N, KV, GQ, hd, W)

    nb, d_q = w_q.shape[0], w_q.shape[1]
    wq_slot = jnp.swapaxes(w_q.astype(BF16).reshape(nb, d_q, KV, GQ, hd), 2, 3).reshape(nb, d_q, nh * hd)
    wo_slot = jnp.swapaxes(w_o.astype(BF16).reshape(nb, KV, GQ, hd, w_o.shape[2]), 1, 2).reshape(nb, nh * hd, w_o.shape[2])
    sinks_slot = jnp.swapaxes(attn_sinks.reshape(nb, KV, GQ), 1, 2).reshape(nb, nh)
    row1 = lambda v: v[:, None, :]
    pad_heads = lambda v: jnp.pad(v, ((0, 0),) * (v.ndim - 1) + ((0, LANES - heads),))
    dk = KV * hd
    w = {
        'ffn1_norm': row1(ffn1_norm), 'ffn1_wi': ffn1_wi.astype(BF16), 'ffn1_wo': ffn1_wo.astype(BF16),
        'ffn2_norm': row1(ffn2_norm), 'ffn2_wi': ffn2_wi.astype(BF16), 'ffn2_wo': ffn2_wo.astype(BF16),
        'mix_norm': row1(mix_norm), 'ple_norm': row1(ple_norm),
        'ple_gate': ple_gate.astype(BF16), 'ple_proj': ple_proj.astype(BF16),
        'ssm_in': ssm_in.astype(BF16), 'ssm_wd': pad_heads(ssm_in[:, :, di + cd:]).astype(BF16),
        'ssm_conv_w': ssm_conv_w, 'ssm_conv_b': row1(ssm_conv_b),
        'ssm_dt_bias': row1(pad_heads(ssm_dt_bias)), 'ssm_a_log': row1(pad_heads(ssm_a_log)),
        'ssm_dsk': row1(jnp.repeat(ssm_d, P, axis=1)),
        'ssm_norm': row1(ssm_norm), 'ssm_out': ssm_out.astype(BF16),
        'kv_norm': kv_norm[None, :], 'w_k': w_kv[:, :dk].astype(BF16), 'w_v': w_kv[:, dk:].astype(BF16),
        'k_norm': jnp.tile(k_norm, KV)[None, :],
        'w_q': wq_slot, 'q_norm': row1(jnp.tile(q_norm, (1, nh))),
        'sinks_row': row1(jnp.pad(sinks_slot, ((0, 0), (0, LANES - nh)))),
        'sinks_col': sinks_slot[:, :, None],
        'w_o': wo_slot,
    }
    bs = x_sample.shape[0]
    y_p, ssm_p, conv_p, k_p, v_p = _trunk(x_prompt, p_prompt, None, None, None, None, 0, w, dims, True)
    y_s, ssm_s, conv_s, k_s, v_s = _trunk(x_sample, p_sample, state_ssm.reshape(n_a, bs, heads * P, N),
                                          state_conv.reshape(n_a, bs, (CONV_TAPS - 1) * cd),
                                          cache_k.reshape(bs, W, dk), cache_v.reshape(bs, W, dk), PAST_LEN, w, dims, False)
    return (y_p, y_s, ssm_p, conv_p, k_p, v_p, ssm_s, conv_s, k_s, v_s)
```
